```python
import math, functools
import jax, jax.numpy as jnp
from jax import lax
import numpy as np


D_MODEL = 2048
BATCH = 2
SEQ = 4096
DEPTH = 4
DEC_BATCH = 8
DEC_SEQ = 16
PAST_LEN = 4096

CHUNK = 64

GLA_HEADS = 4
GLA_WIDTH = D_MODEL // 2
GLA_DV = GLA_WIDTH // GLA_HEADS
GLA_DK = GLA_DV // 2
GATE_RANK = 16
GATE_TAU = 16.0

POOL_WINDOWS = (2, 4, 8, 16)
POOL_GROUPS = len(POOL_WINDOWS)
POOL_WIDTH = D_MODEL - GLA_WIDTH
POOL_GC = POOL_WIDTH // POOL_GROUPS
MAX_WIN = max(POOL_WINDOWS)
POOL_HIST = MAX_WIN - 1

MIX_WIDTH = GLA_WIDTH + POOL_WIDTH

Q_OFF = 0
K_OFF = Q_OFF + GLA_HEADS * GLA_DK
V_OFF = K_OFF + GLA_HEADS * GLA_DK
G_OFF = V_OFF + GLA_WIDTH
A_OFF = G_OFF + GLA_WIDTH
P_OFF = A_OFF + GATE_RANK
IN_COLS = P_OFF + POOL_WIDTH

N_GROUPS = 4
N_EXP = 8
EXPERT_HIDDEN = D_MODEL // 4
TOP_K_INNER = 2

DN_ALPHA = (2 * DEPTH) ** 0.25
DN_BETA = (8 * DEPTH) ** -0.25
LN_EPS = 1e-5
RMS_EPS = 1e-6

kernel_name = 'hymba_gla_pool_hmoe_deepnorm_stream_step'


def layer_norm(x, g, b):
    xf = x.astype(jnp.float32)
    mu = jnp.mean(xf, axis=-1, keepdims=True)
    var = jnp.mean(jnp.square(xf - mu), axis=-1, keepdims=True)
    y = (xf - mu) * lax.rsqrt(var + LN_EPS) * g.astype(jnp.float32) + b.astype(jnp.float32)
    return y.astype(x.dtype)


def gla_recurrence(q, k, v, log_a, s0):
    bsz, t_len, nh, dk = q.shape
    dv = v.shape[-1]
    blk = min(t_len, CHUNK)
    n_blk = t_len // blk

    def to_blocks(t):
        return t.astype(jnp.float32).reshape(bsz, n_blk, blk, nh, t.shape[-1]).transpose(1, 0, 3, 2, 4)

    qb, kb, vb, ab = to_blocks(q), to_blocks(k), to_blocks(v), to_blocks(log_a)
    mask = jnp.tril(jnp.ones((blk, blk), dtype=bool))

    def step(s, inp):
        qc, kc, vc, ac = inp
        bcum = jnp.cumsum(ac, axis=2)
        q_t = qc * jnp.exp(bcum)
        k_t = kc * jnp.exp(-bcum)
        att = jnp.where(mask, jnp.einsum('bhtk,bhsk->bhts', q_t, k_t), 0.0)
        o = jnp.einsum('bhts,bhsv->bhtv', att, vc) + jnp.einsum('bhtk,bhkv->bhtv', q_t, s)
        b_last = bcum[:, :, -1, :]
        k_dec = kc * jnp.exp(b_last[:, :, None, :] - bcum)
        s = jnp.exp(b_last)[..., None] * s + jnp.einsum('bhsk,bhsv->bhkv', k_dec, vc)
        return s, o

    s_fin, o = lax.scan(step, s0.astype(jnp.float32), (qb, kb, vb, ab))
    o = o.transpose(1, 0, 3, 2, 4).reshape(bsz, t_len, nh, dv)
    return o, s_fin


def pool_mix(p, hist, w_pool, pool_scale):
    bsz, t_len, ch = p.shape
    n_hist = hist.shape[1]
    z = jnp.concatenate([hist.astype(p.dtype), p], axis=1).astype(jnp.float32)
    zp = jnp.pad(z, ((0, 0), (MAX_WIN, 0), (0, 0)))
    cs = jnp.concatenate([jnp.zeros((bsz, 1, ch), jnp.float32), jnp.cumsum(zp, axis=1)], axis=1)
    hi = cs[:, MAX_WIN + n_hist + 1: MAX_WIN + n_hist + t_len + 1]
    avail = n_hist + jnp.arange(t_len) + 1
    means = []
    for gi, w in enumerate(POOL_WINDOWS):
        sl = slice(gi * POOL_GC, (gi + 1) * POOL_GC)
        lo = cs[:, MAX_WIN + n_hist + 1 - w: MAX_WIN + n_hist + t_len + 1 - w, sl]
        cnt = jnp.minimum(avail, w).astype(jnp.float32)[None, :, None]
        means.append((hi[..., sl] - lo) / cnt)
    m = jnp.concatenate(means, axis=-1) - z[:, n_hist:]
    m = m.astype(p.dtype).reshape(bsz, t_len, POOL_GROUPS, POOL_GC)
    y = jnp.einsum('btgc,gcd->btgd', m, w_pool).reshape(bsz, t_len, ch) * pool_scale
    new_hist = z[:, -POOL_HIST:].astype(p.dtype)
    return y, new_hist


def token_mixer(h, s0, hist, w_in, w_forget_up, b_forget, gla_norm_g, w_pool, pool_scale, w_out):
    bsz, t_len, _ = h.shape
    u = h @ w_in
    q = u[..., Q_OFF:K_OFF].reshape(bsz, t_len, GLA_HEADS, GLA_DK) * (GLA_DK ** -0.5)
    k = u[..., K_OFF:V_OFF].reshape(bsz, t_len, GLA_HEADS, GLA_DK)
    v = u[..., V_OFF:G_OFF].reshape(bsz, t_len, GLA_HEADS, GLA_DV)
    g = u[..., G_OFF:A_OFF]
    a_code = u[..., A_OFF:P_OFF]
    p = u[..., P_OFF:]
    log_a = jax.nn.log_sigmoid((a_code @ w_forget_up + b_forget).astype(jnp.float32)) / GATE_TAU
    log_a = log_a.reshape(bsz, t_len, GLA_HEADS, GLA_DK)
    o, s_fin = gla_recurrence(q, k, v, log_a, s0)
    o = o * lax.rsqrt(jnp.mean(jnp.square(o), axis=-1, keepdims=True) + RMS_EPS)
    o = o.reshape(bsz, t_len, GLA_WIDTH) * gla_norm_g.astype(jnp.float32)
    gla_out = (o * jax.nn.silu(g.astype(jnp.float32))).astype(h.dtype)
    pool_out, new_hist = pool_mix(p, hist, w_pool, pool_scale)
    y = jnp.concatenate([gla_out, pool_out.astype(h.dtype)], axis=-1) @ w_out
    return y, s_fin.astype(s0.dtype), new_hist


def hier_moe(h, wg, bg, wf, bf, w1, w3, w2):
    bsz, t_len, d = h.shape
    xf = h.reshape(-1, d)
    lg = (xf @ wg + bg).astype(jnp.float32)
    pg = jax.nn.softmax(lg, axis=-1)
    top_pg, gsel = lax.top_k(pg, 1)
    lf_all = (jnp.einsum('nd,gde->nge', xf, wf) + bf).astype(jnp.float32)
    lf = jnp.take_along_axis(lf_all, gsel[:, :, None], axis=1)[:, 0]
    vals, eidx = lax.top_k(lf, TOP_K_INNER)
    wts = jax.nn.softmax(vals, axis=-1) * top_pg
    e_w = jnp.sum(jax.nn.one_hot(eidx, N_EXP, dtype=jnp.float32) * wts[..., None], axis=1)
    comb = (jax.nn.one_hot(gsel[:, 0], N_GROUPS, dtype=jnp.float32)[:, :, None] * e_w[:, None, :]).astype(h.dtype)
    y = jnp.zeros_like(xf)
    for gi in range(N_GROUPS):
        a = jax.nn.silu(jnp.einsum('nd,edf->nef', xf, w1[gi])) * jnp.einsum('nd,edf->nef', xf, w3[gi])
        y = y + jnp.einsum('nef,efd->nd', a * comb[:, gi, :, None], w2[gi])
    return y.reshape(bsz, t_len, d)


def run_trunk(x, gla_s0, pool_hist, ln_in_g, ln_in_b, w_in, w_forget_up, b_forget, gla_norm_g,
              w_pool, pool_scale, w_out, ln1_g, ln1_b, router_group_w, router_group_b,
              router_expert_w, router_expert_b, w_exp_gate, w_exp_up, w_exp_down, ln2_g, ln2_b):
    x = layer_norm(x, ln_in_g, ln_in_b)
    states, hists = [], []
    for l in range(DEPTH):
        y, s_l, h_l = token_mixer(x, gla_s0[l], pool_hist[l], w_in[l], w_forget_up[l], b_forget[l],
                                  gla_norm_g[l], w_pool[l], pool_scale[l], w_out[l])
        x = layer_norm(DN_ALPHA * x + y, ln1_g[l], ln1_b[l])
        f = hier_moe(x, router_group_w[l], router_group_b[l], router_expert_w[l], router_expert_b[l],
                     w_exp_gate[l], w_exp_up[l], w_exp_down[l])
        x = layer_norm(DN_ALPHA * x + f, ln2_g[l], ln2_b[l])
        states.append(s_l)
        hists.append(h_l)
    return x, jnp.stack(states), jnp.stack(hists)


def setup_inputs(seed: int = 0) -> dict:
    key = jax.random.key(seed)
    ks = jax.random.split(key, 26)
    f32 = jnp.float32

    def nrm(k, shape, s):
        return jax.random.normal(k, shape, f32) * s

    d = D_MODEL
    col_scale = jnp.ones((IN_COLS,), f32).at[V_OFF:G_OFF].set(DN_BETA)
    return {
        'x_prompt': nrm(ks[0], (BATCH, SEQ, d), 1.0),
        'x_sample': nrm(ks[1], (DEC_BATCH, DEC_SEQ, d), 1.0),
        'state_gla': nrm(ks[2], (DEPTH, DEC_BATCH, GLA_HEADS, GLA_DK, GLA_DV), 0.1),
        'cache_pool': nrm(ks[3], (DEPTH, DEC_BATCH, POOL_HIST, POOL_WIDTH), 1.0),
        'ln_in_g': 1.0 + nrm(ks[4], (d,), 0.02),
        'ln_in_b': nrm(ks[5], (d,), 0.02),
        'w_in': nrm(ks[6], (DEPTH, d, IN_COLS), d ** -0.5) * col_scale,
        'w_forget_up': nrm(ks[7], (DEPTH, GATE_RANK, GLA_HEADS * GLA_DK), GATE_RANK ** -0.5),
        'b_forget': nrm(ks[8], (DEPTH, GLA_HEADS * GLA_DK), 0.1),
        'gla_norm_g': 1.0 + nrm(ks[9], (DEPTH, GLA_WIDTH), 0.02),
        'w_pool': nrm(ks[10], (DEPTH, POOL_GROUPS, POOL_GC, POOL_GC), POOL_GC ** -0.5),
        'pool_scale': 1.0 + nrm(ks[11], (DEPTH, POOL_WIDTH), 0.02),
        'w_out': nrm(ks[12], (DEPTH, MIX_WIDTH, d), MIX_WIDTH ** -0.5 * DN_BETA),
        'ln1_g': 1.0 + nrm(ks[13], (DEPTH, d), 0.02),
        'ln1_b': nrm(ks[14], (DEPTH, d), 0.02),
        'router_group_w': nrm(ks[15], (DEPTH, d, N_GROUPS), d ** -0.5),
        'router_group_b': nrm(ks[16], (DEPTH, N_GROUPS), 0.01),
        'router_expert_w': nrm(ks[17], (DEPTH, N_GROUPS, d, N_EXP), d ** -0.5),
        'router_expert_b': nrm(ks[18], (DEPTH, N_GROUPS, N_EXP), 0.01),
        'w_exp_gate': nrm(ks[19], (DEPTH, N_GROUPS, N_EXP, d, EXPERT_HIDDEN), d ** -0.5),
        'w_exp_up': nrm(ks[20], (DEPTH, N_GROUPS, N_EXP, d, EXPERT_HIDDEN), d ** -0.5),
        'w_exp_down': nrm(ks[21], (DEPTH, N_GROUPS, N_EXP, EXPERT_HIDDEN, d), EXPERT_HIDDEN ** -0.5 * DN_BETA),
        'ln2_g': 1.0 + nrm(ks[22], (DEPTH, d), 0.02),
        'ln2_b': nrm(ks[23], (DEPTH, d), 0.02),
    }


def reference(x_prompt, x_sample, state_gla, cache_pool, ln_in_g, ln_in_b, w_in, w_forget_up, b_forget,
              gla_norm_g, w_pool, pool_scale, w_out, ln1_g, ln1_b, router_group_w, router_group_b,
              router_expert_w, router_expert_b, w_exp_gate, w_exp_up, w_exp_down, ln2_g, ln2_b):
    trunk = functools.partial(
        run_trunk, ln_in_g=ln_in_g, ln_in_b=ln_in_b, w_in=w_in, w_forget_up=w_forget_up,
        b_forget=b_forget, gla_norm_g=gla_norm_g, w_pool=w_pool, pool_scale=pool_scale, w_out=w_out,
        ln1_g=ln1_g, ln1_b=ln1_b, router_group_w=router_group_w, router_group_b=router_group_b,
        router_expert_w=router_expert_w, router_expert_b=router_expert_b, w_exp_gate=w_exp_gate,
        w_exp_up=w_exp_up, w_exp_down=w_exp_down, ln2_g=ln2_g, ln2_b=ln2_b)
    n_prompt = x_prompt.shape[0]
    s0_prompt = jnp.zeros((DEPTH, n_prompt, GLA_HEADS, GLA_DK, GLA_DV), state_gla.dtype)
    hist_prompt = jnp.zeros((DEPTH, n_prompt, 0, POOL_WIDTH), cache_pool.dtype)
    y_prompt, state_gla_p, cache_pool_p = trunk(x_prompt, s0_prompt, hist_prompt)
    y_sample, state_gla_s, cache_pool_s = trunk(x_sample, state_gla, cache_pool)
    return (y_prompt, y_sample, state_gla_p, cache_pool_p, state_gla_s, cache_pool_s)
```

```python
import functools

import jax
import jax.numpy as jnp
from jax import lax
from jax.experimental import pallas as pl
from jax.experimental.pallas import tpu as pltpu

D_MODEL = 2048
DEPTH = 4
CHUNK = 64

GLA_HEADS = 4
GLA_WIDTH = D_MODEL // 2
GLA_DV = GLA_WIDTH // GLA_HEADS
GLA_DK = GLA_DV // 2
GLA_KW = GLA_HEADS * GLA_DK
GATE_RANK = 16
GATE_TAU = 16.0

POOL_WINDOWS = (2, 4, 8, 16)
POOL_GROUPS = len(POOL_WINDOWS)
POOL_WIDTH = D_MODEL - GLA_WIDTH
POOL_GC = POOL_WIDTH // POOL_GROUPS
POOL_HIST = max(POOL_WINDOWS) - 1
HIST_ROWS = POOL_HIST + 1

Q_OFF = 0
K_OFF = Q_OFF + GLA_KW
V_OFF = K_OFF + GLA_KW
G_OFF = V_OFF + GLA_WIDTH
A_OFF = G_OFF + GLA_WIDTH
P_OFF = A_OFF + GATE_RANK
IN_COLS = P_OFF + POOL_WIDTH
PA_COLS = POOL_WIDTH + 128

N_GROUPS = 4
N_EXP = 8
N_EXPERTS = N_GROUPS * N_EXP
EXPERT_HIDDEN = D_MODEL // 4
ROUTE_LANES = 128
ROUTER_CORRECTION = 0.5

DN_ALPHA = (2 * DEPTH) ** 0.25
LN_EPS = 1e-5
RMS_EPS = 1e-6

VMEM_LIMIT_BYTES = 56 * 1024 * 1024
LANE = 128
EXPERT_TILE = 256
BF16 = jnp.bfloat16
F32 = jnp.float32


def _params(sem, vmem=VMEM_LIMIT_BYTES):
    return pltpu.CompilerParams(dimension_semantics=sem, vmem_limit_bytes=vmem)


def _row_tile(n, prefs=(640, 512, 320, 256, 128)):
    for t in prefs:
        if n % t == 0:
            return t
    raise ValueError(f"no row tile for {n} rows")


def _layer_norm(x, g, b):
    mu = jnp.mean(x, axis=-1, keepdims=True)
    xc = x - mu
    var = jnp.mean(xc * xc, axis=-1, keepdims=True)
    return xc * lax.rsqrt(var + LN_EPS) * g + b


def _dot(a, b):
    return jnp.dot(a, b, preferred_element_type=F32)


def _dot_nt(a, b):
    return lax.dot_general(a, b, (((1,), (1,)), ((), ())), preferred_element_type=F32)


def _dot_split(m_bf16, x):
    hi = x.astype(BF16)
    lo = (x - hi.astype(F32)).astype(BF16)
    return _dot(m_bf16, hi) + _dot(m_bf16, lo)


def _ln_in_kernel(xp_ref, xs_ref, g_ref, b_ref, x_ref, xb_ref, *, n_prompt_blocks):
    i = pl.program_id(0)
    x = jnp.where(i < n_prompt_blocks, xp_ref[...], xs_ref[...])
    y = _layer_norm(x, g_ref[...], b_ref[...])
    x_ref[...] = y
    xb_ref[...] = y.astype(BF16)


def _ln_in(xp, xs, g, b):
    n_p, n_s = xp.shape[0], xs.shape[0]
    tm = n_s
    assert n_p % tm == 0
    npb = n_p // tm
    n = n_p + n_s
    return pl.pallas_call(
        functools.partial(_ln_in_kernel, n_prompt_blocks=npb),
        grid=(npb + 1,),
        in_specs=[
            pl.BlockSpec((tm, D_MODEL), lambda i: (jnp.minimum(i, npb - 1), 0)),
            pl.BlockSpec((tm, D_MODEL), lambda i: (0, 0)),
            pl.BlockSpec((1, D_MODEL), lambda i: (0, 0)),
            pl.BlockSpec((1, D_MODEL), lambda i: (0, 0)),
        ],
        out_specs=[
            pl.BlockSpec((tm, D_MODEL), lambda i: (i, 0)),
            pl.BlockSpec((tm, D_MODEL), lambda i: (i, 0)),
        ],
        out_shape=[jax.ShapeDtypeStruct((n, D_MODEL), F32), jax.ShapeDtypeStruct((n, D_MODEL), BF16)],
        compiler_params=_params(("arbitrary",)),
        name="ln_in",
    )(xp, xs, g, b)


def _proj_kernel(x_ref, w_ref, o_ref, wb_ref):
    @pl.when(pl.program_id(1) == 0)
    def _():
        wb_ref[...] = w_ref[...].astype(BF16)

    o_ref[...] = _dot(x_ref[...], wb_ref[...]).astype(o_ref.dtype)


def _in_proj_qkvg(xb, w_in, layer):
    n = xb.shape[0]
    tm, tn = _row_tile(n), 1024
    return pl.pallas_call(
        _proj_kernel,
        grid=(A_OFF // tn, n // tm),
        in_specs=[
            pl.BlockSpec((tm, D_MODEL), lambda j, m: (m, 0)),
            pl.BlockSpec((None, D_MODEL, tn), lambda j, m: (layer, 0, j)),
        ],
        out_specs=pl.BlockSpec((tm, tn), lambda j, m: (m, j)),
        out_shape=jax.ShapeDtypeStruct((n, A_OFF), BF16),
        scratch_shapes=[pltpu.VMEM((D_MODEL, tn), BF16)],
        compiler_params=_params(("arbitrary", "arbitrary")),
        name="in_proj_qkvg",
    )(xb, w_in)


def _in_proj_pa(xb, w_pa, layer):
    n = xb.shape[0]
    tm = _row_tile(n)
    return pl.pallas_call(
        _proj_kernel,
        grid=(1, n // tm),
        in_specs=[
            pl.BlockSpec((tm, D_MODEL), lambda j, m: (m, 0)),
            pl.BlockSpec((None, D_MODEL, PA_COLS), lambda j, m: (layer, 0, 0)),
        ],
        out_specs=pl.BlockSpec((tm, PA_COLS), lambda j, m: (m, 0)),
        out_shape=jax.ShapeDtypeStruct((n, PA_COLS), F32),
        scratch_shapes=[pltpu.VMEM((D_MODEL, PA_COLS), BF16)],
        compiler_params=_params(("arbitrary", "arbitrary")),
        name="in_proj_pa",
    )(xb, w_pa)


def _mixer_kernel(*refs, n_seq, n_blk, n_chunks, valid, n_hist, has_init, fill_tail):
    if has_init:
        (qkvg_ref, pa_ref, wfu_ref, bf_ref, gain_ref, wpool_ref, pscale_ref, s0_ref, h0_ref, _mixed_in,
         mixed_ref, sout_ref, hout_ref, s_scr, z_scr) = refs
    else:
        (qkvg_ref, pa_ref, wfu_ref, bf_ref, gain_ref, wpool_ref, pscale_ref,
         mixed_ref, sout_ref, hout_ref, s_scr, z_scr) = refs
    C = CHUNK
    step = pl.program_id(0)
    blk = step % n_blk

    def pad_rows(x):
        if valid == C:
            return x
        return jnp.concatenate([x, jnp.zeros((C - valid, x.shape[1]), x.dtype)], axis=0)

    def process():
        @pl.when(blk == 0)
        def _():
            if has_init:
                s_scr[...] = s0_ref[...]
                z_scr[0:C - HIST_ROWS, :] = jnp.zeros((C - HIST_ROWS, POOL_WIDTH), F32)
                z_scr[C - HIST_ROWS:C, :] = h0_ref[...]
            else:
                s_scr[...] = jnp.zeros(s_scr.shape, F32)
                z_scr[...] = jnp.zeros(z_scr.shape, F32)

        row = lax.broadcasted_iota(jnp.int32, (C, C), 0)
        col = lax.broadcasted_iota(jnp.int32, (C, C), 1)
        causal = row >= col
        tri = causal.astype(BF16)
        t_idx = lax.broadcasted_iota(jnp.int32, (C, 2 * C), 0)
        s_idx = lax.broadcasted_iota(jnp.int32, (C, 2 * C), 1) - C
        lag = t_idx - s_idx
        bands = [((lag >= 0) & (lag < w)).astype(BF16) for w in POOL_WINDOWS]

        def chunk(j, carry):
            if n_chunks == 1:
                rows = slice(0, valid)
            else:
                rows = pl.ds(pl.multiple_of(j * C, C), C)
            qkvg = pad_rows(qkvg_ref[rows, :])
            pa = pad_rows(pa_ref[rows, :])
            q = qkvg[:, Q_OFF:K_OFF].astype(F32)
            k = qkvg[:, K_OFF:V_OFF].astype(F32)
            v = qkvg[:, V_OFF:G_OFF]
            g = qkvg[:, G_OFF:A_OFF].astype(F32)
            zc = pa[:, :POOL_WIDTH]
            a = pa[:, POOL_WIDTH:]

            la = jax.nn.log_sigmoid(_dot(a.astype(BF16), wfu_ref[...]) + bf_ref[...]) * (1.0 / GATE_TAU)
            if valid != C:
                la = jnp.where(lax.broadcasted_iota(jnp.int32, la.shape, 0) < valid, la, 0.0)
            bcum = _dot_split(tri, la)
            b_last = bcum[C - 1:C, :]
            q_t = (q * jnp.exp(bcum) * (GLA_DK ** -0.5)).astype(BF16)
            k_t = (k * jnp.exp(-bcum)).astype(BF16)
            k_dec = k * jnp.exp(b_last - bcum)
            e_last = jnp.exp(b_last)
            outs = []
            for h in range(GLA_HEADS):
                ks = slice(h * GLA_DK, (h + 1) * GLA_DK)
                vs = slice(h * GLA_DV, (h + 1) * GLA_DV)
                s_h = s_scr[h]
                att = jnp.where(causal, _dot_nt(q_t[:, ks], k_t[:, ks]), 0.0)
                o = _dot(att.astype(BF16), v[:, vs]) + _dot(q_t[:, ks], s_h.astype(BF16))
                k_aug = jnp.concatenate([k_dec[:, ks], jnp.zeros((GLA_DK - C, GLA_DK), F32)], axis=0)
                v_aug = jnp.concatenate([v[:, vs], jnp.zeros((GLA_DK - C, GLA_DV), BF16)], axis=0)
                decay = jnp.broadcast_to(e_last[:, ks], (GLA_DK, GLA_DK)).T
                decay = jnp.concatenate([decay] * (GLA_DV // GLA_DK), axis=1)
                s_scr[h] = decay * s_h + _dot(k_aug.T.astype(BF16), v_aug)
                outs.append(o * lax.rsqrt(jnp.mean(o * o, axis=-1, keepdims=True) + RMS_EPS))
            gla = jnp.concatenate(outs, axis=1) * gain_ref[...] * (g * jax.nn.sigmoid(g))

            z_ext = jnp.concatenate([z_scr[...], zc], axis=0)
            t0 = (blk * n_chunks + j) * C + n_hist + 1
            avail = lax.broadcasted_iota(jnp.int32, (C, POOL_GC), 0) + t0
            pooled = []
            for gi, w in enumerate(POOL_WINDOWS):
                cs = slice(gi * POOL_GC, (gi + 1) * POOL_GC)
                cnt = jnp.minimum(avail, w).astype(F32)
                m = _dot_split(bands[gi], z_ext[:, cs]) / cnt - zc[:, cs]
                pooled.append(_dot(m.astype(BF16), wpool_ref[gi]))
            pool = jnp.concatenate(pooled, axis=1) * pscale_ref[...]
            z_scr[...] = zc

            res = jnp.concatenate([gla, pool], axis=1).astype(BF16)
            mixed_ref[rows, :] = res[0:valid]
            return carry

        if n_chunks == 1:
            chunk(0, 0)
        else:
            lax.fori_loop(0, n_chunks, chunk, 0)

        @pl.when(blk == n_blk - 1)
        def _():
            sout_ref[...] = s_scr[...]
            hout_ref[...] = z_scr[valid - HIST_ROWS:valid, :]

    if fill_tail:
        pl.when(step < n_seq * n_blk)(process)

        @pl.when(step == n_seq * n_blk)
        def _():
            mixed_ref[...] = jnp.zeros(mixed_ref.shape, BF16)
    else:
        process()


def _mixer(qkvg, pa, wfu, bfg, gain, wpool, pscale, *, row0, n_seq, seq_len, n_hist, init=None, mixed_in=None):
    n = qkvg.shape[0]
    if seq_len % CHUNK == 0:
        rb, valid = _row_tile(seq_len, (512, 256, 128, 64)), CHUNK
    else:
        rb, valid = seq_len, seq_len
    assert seq_len % rb == 0 and row0 % rb == 0 and valid >= HIST_ROWS
    n_blk = seq_len // rb
    n_chunks = rb // valid
    base = row0 // rb
    n_steps = n_seq * n_blk
    has_init = init is not None
    assert has_init == (mixed_in is not None)
    tail_rows = n - (row0 + n_seq * seq_len)
    fill_tail = (not has_init) and tail_rows > 0
    assert tail_rows <= rb

    rows = lambda t: (base + t, 0)
    const2 = lambda t: (0, 0)
    seq3 = lambda t: (jnp.minimum(t // n_blk, n_seq - 1), 0, 0)
    seq4 = lambda t: (jnp.minimum(t // n_blk, n_seq - 1), 0, 0, 0)

    in_specs = [
        pl.BlockSpec((rb, A_OFF), rows),
        pl.BlockSpec((rb, PA_COLS), rows),
        pl.BlockSpec((LANE, GLA_KW), const2),
        pl.BlockSpec((1, GLA_KW), const2),
        pl.BlockSpec((1, GLA_WIDTH), const2),
        pl.BlockSpec((POOL_GROUPS, POOL_GC, POOL_GC), lambda t: (0, 0, 0)),
        pl.BlockSpec((1, POOL_WIDTH), const2),
    ]
    args = [qkvg, pa, wfu, bfg, gain, wpool, pscale]
    aliases = {}
    if has_init:
        in_specs += [
            pl.BlockSpec((None, GLA_HEADS, GLA_DK, GLA_DV), seq4),
            pl.BlockSpec((None, HIST_ROWS, POOL_WIDTH), seq3),
            pl.BlockSpec(memory_space=pl.ANY),
        ]
        args += [init[0], init[1], mixed_in]
        aliases = {len(args) - 1: 0}
    return pl.pallas_call(
        functools.partial(_mixer_kernel, n_seq=n_seq, n_blk=n_blk, n_chunks=n_chunks, valid=valid, n_hist=n_hist,
                          has_init=has_init, fill_tail=fill_tail),
        grid=(n_steps + int(fill_tail),),
        in_specs=in_specs,
        out_specs=[
            pl.BlockSpec((rb, D_MODEL), rows),
            pl.BlockSpec((None, GLA_HEADS, GLA_DK, GLA_DV), seq4),
            pl.BlockSpec((None, HIST_ROWS, POOL_WIDTH), seq3),
        ],
        out_shape=[
            jax.ShapeDtypeStruct((n, D_MODEL), BF16),
            jax.ShapeDtypeStruct((n_seq, GLA_HEADS, GLA_DK, GLA_DV), F32),
            jax.ShapeDtypeStruct((n_seq, HIST_ROWS, POOL_WIDTH), F32),
        ],
        scratch_shapes=[
            pltpu.VMEM((GLA_HEADS, GLA_DK, GLA_DV), F32),
            pltpu.VMEM((CHUNK, POOL_WIDTH), F32),
        ],
        input_output_aliases=aliases,
        compiler_params=_params(("arbitrary",)),
        name="mixer_init" if has_init else "mixer",
    )(*args)


def _route(logits):
    lane = lax.broadcasted_iota(jnp.int32, logits.shape, 1).astype(F32)
    neg = -jnp.inf
    first_lane = lambda hit: jnp.min(jnp.where(hit, lane, float(ROUTE_LANES)), axis=-1, keepdims=True)
    lg = jnp.where(lane < N_GROUPS, logits, neg)
    un = jnp.exp(lg - jnp.max(lg, axis=-1, keepdims=True))
    pg = un / jnp.sum(un, axis=-1, keepdims=True)
    top_pg = jnp.max(pg, axis=-1, keepdims=True)
    gsel = first_lane((pg == top_pg) & (lane < N_GROUPS))
    first = N_GROUPS + gsel * N_EXP
    le = jnp.where((lane >= first) & (lane < first + N_EXP), logits, neg)
    v0 = jnp.max(le, axis=-1, keepdims=True)
    i0 = first_lane(le == v0)
    le = jnp.where(lane == i0, neg, le)
    v1 = jnp.max(le, axis=-1, keepdims=True)
    i1 = first_lane(le == v1)
    u1 = jnp.exp(v1 - v0)
    den = 1.0 + u1
    e0 = (i0 - N_GROUPS).astype(jnp.int32)
    e1 = (i1 - N_GROUPS).astype(jnp.int32)
    return e0, e1, (1.0 / den) * top_pg, (u1 / den) * top_pg


def _out_proj_kernel(mixed_ref, x_ref, w_ref, g_ref, b_ref, wr_ref, br_ref, x1_ref, ri_ref, rw_ref, wb_ref, wrs_ref):
    @pl.when(pl.program_id(0) == 0)
    def _():
        wb_ref[...] = w_ref[...].astype(BF16)
        wr = wr_ref[...]
        wr_hi = wr.astype(BF16)
        wrs_ref[0] = wr_hi
        wrs_ref[1] = (wr - wr_hi.astype(F32)).astype(BF16)

    y = _dot(mixed_ref[...], wb_ref[...])
    x1 = _layer_norm(DN_ALPHA * x_ref[...] + y, g_ref[...], b_ref[...])
    x1_ref[...] = x1
    x_hi = x1.astype(BF16)
    x_lo = (x1 - x_hi.astype(F32)).astype(BF16)
    corr = _dot(x_lo, wrs_ref[0]) + _dot(x_hi, wrs_ref[1])
    logits = _dot(x_hi, wrs_ref[0]) + ROUTER_CORRECTION * corr + br_ref[...]
    e0, e1, w0, w1 = _route(logits)
    lane = lax.broadcasted_iota(jnp.int32, logits.shape, 1)
    ri_ref[...] = jnp.where(lane == 0, e0, jnp.where(lane == 1, e1, 0))
    rw_ref[...] = jnp.where(lane == 0, w0, jnp.where(lane == 1, w1, 0.0))


def _out_proj(mixed, x, w_out, g, b, wr, br, layer):
    n = x.shape[0]
    tm = _row_tile(n, (320, 256, 128))
    row = lambda i: (i, 0)
    const = lambda i: (0, 0)
    return pl.pallas_call(
        _out_proj_kernel,
        grid=(n // tm,),
        in_specs=[
            pl.BlockSpec((tm, D_MODEL), row),
            pl.BlockSpec((tm, D_MODEL), row),
            pl.BlockSpec((None, D_MODEL, D_MODEL), lambda i: (layer, 0, 0), pipeline_mode=pl.Buffered(1)),
            pl.BlockSpec((1, D_MODEL), const),
            pl.BlockSpec((1, D_MODEL), const),
            pl.BlockSpec((D_MODEL, ROUTE_LANES), const),
            pl.BlockSpec((1, ROUTE_LANES), const),
        ],
        out_specs=[
            pl.BlockSpec((tm, D_MODEL), row),
            pl.BlockSpec((tm, ROUTE_LANES), row),
            pl.BlockSpec((tm, ROUTE_LANES), row),
        ],
        out_shape=[
            jax.ShapeDtypeStruct((n, D_MODEL), F32),
            jax.ShapeDtypeStruct((n, ROUTE_LANES), jnp.int32),
            jax.ShapeDtypeStruct((n, ROUTE_LANES), F32),
        ],
        scratch_shapes=[pltpu.VMEM((D_MODEL, D_MODEL), BF16), pltpu.VMEM((2, D_MODEL, ROUTE_LANES), BF16)],
        compiler_params=_params(("arbitrary",)),
        name="out_proj_ln_route",
    )(mixed, x, w_out, g, b, wr, br)


def _dispatch_plan(eid):
    n_pairs = eid.shape[0] * 2
    n_tiles = n_pairs // EXPERT_TILE
    flat = eid.reshape(-1)
    onehot = (flat[:, None] == jnp.arange(N_EXPERTS, dtype=jnp.int32)[None, :]).astype(jnp.int32)
    csum = jnp.cumsum(onehot, axis=0)
    rank = jnp.sum(csum * onehot, axis=1) - 1
    counts = csum[-1]
    offs = jnp.concatenate([jnp.zeros((1,), jnp.int32), jnp.cumsum(counts)])
    pos = (offs[flat] + rank).astype(jnp.int32)
    bounds = jnp.sort(jnp.concatenate([jnp.arange(n_tiles, dtype=jnp.int32) * EXPERT_TILE, offs[:N_EXPERTS]]))
    seg_lo = bounds
    seg_hi = jnp.concatenate([bounds[1:], jnp.full((1,), n_pairs, jnp.int32)])
    tile = jnp.minimum(seg_lo // EXPERT_TILE, n_tiles - 1)
    expert = jnp.minimum(jnp.searchsorted(offs[1:], seg_lo, side="right"), N_EXPERTS - 1).astype(jnp.int32)
    lo = seg_lo - tile * EXPERT_TILE
    hi = jnp.where(seg_hi > seg_lo, seg_hi - tile * EXPERT_TILE, lo)
    return pos, tile.astype(jnp.int32), expert, lo.astype(jnp.int32), hi.astype(jnp.int32)


def _row_copy(src_hbm, dst_hbm, sem, src_row, dst_row):
    return pltpu.make_async_copy(src_hbm.at[pl.ds(src_row, 1)], dst_hbm.at[pl.ds(dst_row, 1)], sem)


def _dispatch_kernel(pos_ref, x_hbm, xs_hbm, sem, *, tb):
    base = pl.program_id(0) * tb

    def issue(t, c):
        n = base + t
        _row_copy(x_hbm, xs_hbm, sem, n, pos_ref[2 * n]).start()
        _row_copy(x_hbm, xs_hbm, sem, n, pos_ref[2 * n + 1]).start()
        return c

    lax.fori_loop(0, tb, issue, 0)

    def drain(t, c):
        n = base + t
        _row_copy(x_hbm, xs_hbm, sem, n, pos_ref[2 * n]).wait()
        _row_copy(x_hbm, xs_hbm, sem, n, pos_ref[2 * n + 1]).wait()
        return c

    lax.fori_loop(0, tb, drain, 0)


def _dispatch(pos, x1):
    n = x1.shape[0]
    tb = _row_tile(n)
    return pl.pallas_call(
        functools.partial(_dispatch_kernel, tb=tb),
        grid_spec=pltpu.PrefetchScalarGridSpec(
            num_scalar_prefetch=1,
            grid=(n // tb,),
            in_specs=[pl.BlockSpec(memory_space=pl.ANY)],
            out_specs=pl.BlockSpec(memory_space=pl.ANY),
            scratch_shapes=[pltpu.SemaphoreType.DMA(())],
        ),
        out_shape=jax.ShapeDtypeStruct((2 * n, D_MODEL), F32),
        compiler_params=_params(("arbitrary",)),
        name="moe_dispatch",
    )(pos, x1)


def _experts_kernel(tile_ref, exp_ref, lo_ref, hi_ref, xs_ref, w1_ref, w3_ref, w2_ref, o_ref):
    i = pl.program_id(0)
    lo, hi = lo_ref[i], hi_ref[i]

    def compute():
        x = xs_ref[...].astype(BF16)
        h1 = _dot(x, w1_ref[...].astype(BF16))
        h3 = _dot(x, w3_ref[...].astype(BF16))
        a = (h1 * jax.nn.sigmoid(h1) * h3).astype(BF16)
        return _dot(a, w2_ref[...].astype(BF16))

    @pl.when((hi > lo) & (lo == 0))
    def _():
        o_ref[...] = compute()

    @pl.when((hi > lo) & (lo > 0))
    def _():
        r = lax.broadcasted_iota(jnp.int32, o_ref.shape, 0)
        o_ref[...] = jnp.where((r >= lo) & (r < hi), compute(), o_ref[...])


def _experts(plan, xs, w1, w3, w2, layer):
    tile, expert, lo, hi = plan
    n_items = tile.shape[0]
    wrow = lambda i, t, e, lo_, hi_: (layer * N_EXPERTS + e[i], 0, 0)
    xrow = lambda i, t, e, lo_, hi_: (t[i], 0)
    return pl.pallas_call(
        _experts_kernel,
        grid_spec=pltpu.PrefetchScalarGridSpec(
            num_scalar_prefetch=4,
            grid=(n_items,),
            in_specs=[
                pl.BlockSpec((EXPERT_TILE, D_MODEL), xrow),
                pl.BlockSpec((None, D_MODEL, EXPERT_HIDDEN), wrow),
                pl.BlockSpec((None, D_MODEL, EXPERT_HIDDEN), wrow),
                pl.BlockSpec((None, EXPERT_HIDDEN, D_MODEL), wrow),
            ],
            out_specs=pl.BlockSpec((EXPERT_TILE, D_MODEL), xrow),
        ),
        out_shape=jax.ShapeDtypeStruct(xs.shape, F32),
        compiler_params=_params(("arbitrary",)),
        name="moe_experts",
    )(tile, expert, lo, hi, xs, w1, w3, w2)


def _combine_kernel(pos_ref, x1_ref, rw_ref, ys_hbm, g_ref, b_ref, *rest, tm, n_prompt_blocks):
    if n_prompt_blocks is None:
        x2_ref, x2b_ref, buf, sem = rest
    else:
        yp_ref, ysm_ref, buf, sem = rest
    i = pl.program_id(0)
    base = i * tm

    def gather(t, k):
        src = ys_hbm.at[pl.ds(pos_ref[2 * (base + t) + k], 1)]
        return pltpu.make_async_copy(src, buf.at[k, pl.ds(t, 1)], sem)

    def issue(t, c):
        gather(t, 0).start()
        gather(t, 1).start()
        return c

    lax.fori_loop(0, tm, issue, 0)

    def drain(t, c):
        gather(t, 0).wait()
        gather(t, 1).wait()
        return c

    lax.fori_loop(0, tm, drain, 0)

    rw = rw_ref[...]
    f = rw[:, 0:1] * buf[0] + rw[:, 1:2] * buf[1]
    x2 = _layer_norm(DN_ALPHA * x1_ref[...] + f, g_ref[...], b_ref[...])
    if n_prompt_blocks is None:
        x2_ref[...] = x2
        x2b_ref[...] = x2.astype(BF16)
    else:
        @pl.when(i < n_prompt_blocks)
        def _():
            yp_ref[...] = x2

        @pl.when(i >= n_prompt_blocks)
        def _():
            ysm_ref[...] = x2


def _combine(pos, x1, rw, ys, g, b, *, split=None):
    n = x1.shape[0]
    row = lambda i, p: (i, 0)
    const = lambda i, p: (0, 0)
    if split is None:
        tm, npb = _row_tile(n, (320, 256, 128)), None
        out_specs = [pl.BlockSpec((tm, D_MODEL), row), pl.BlockSpec((tm, D_MODEL), row)]
        out_shape = [jax.ShapeDtypeStruct((n, D_MODEL), F32), jax.ShapeDtypeStruct((n, D_MODEL), BF16)]
    else:
        n_p, n_s = split
        tm = n_s
        assert n_p % tm == 0 and n_p + n_s == n
        npb = n_p // tm
        out_specs = [
            pl.BlockSpec((tm, D_MODEL), lambda i, p: (jnp.minimum(i, npb - 1), 0)),
            pl.BlockSpec((tm, D_MODEL), lambda i, p: (jnp.maximum(i - npb, 0), 0)),
        ]
        out_shape = [jax.ShapeDtypeStruct((n_p, D_MODEL), F32), jax.ShapeDtypeStruct((n_s, D_MODEL), F32)]
    return pl.pallas_call(
        functools.partial(_combine_kernel, tm=tm, n_prompt_blocks=npb),
        grid_spec=pltpu.PrefetchScalarGridSpec(
            num_scalar_prefetch=1,
            grid=(n // tm,),
            in_specs=[
                pl.BlockSpec((tm, D_MODEL), row),
                pl.BlockSpec((tm, ROUTE_LANES), row),
                pl.BlockSpec(memory_space=pl.ANY),
                pl.BlockSpec((1, D_MODEL), const),
                pl.BlockSpec((1, D_MODEL), const),
            ],
            out_specs=out_specs,
            scratch_shapes=[pltpu.VMEM((2, tm, D_MODEL), F32), pltpu.SemaphoreType.DMA(())],
        ),
        out_shape=out_shape,
        compiler_params=_params(("arbitrary",)),
        name="moe_combine_ln" if split is None else "moe_combine_ln_final",
    )(pos, x1, rw, ys, g, b)


def kernel(x_prompt, x_sample, state_gla, cache_pool, ln_in_g, ln_in_b, w_in, w_forget_up, b_forget, gla_norm_g, w_pool, pool_scale, w_out, ln1_g, ln1_b, router_group_w, router_group_b, router_expert_w, router_expert_b, w_exp_gate, w_exp_up, w_exp_down, ln2_g, ln2_b):
    n_pb, seq, d = x_prompt.shape
    n_sb, dseq, _ = x_sample.shape
    n_p, n_s = n_pb * seq, n_sb * dseq
    depth = w_in.shape[0]
    row2 = lambda v: v.reshape(1, -1)

    w_pa = jnp.concatenate(
        [w_in[:, :, P_OFF:], w_in[:, :, A_OFF:P_OFF], jnp.zeros((depth, d, PA_COLS - POOL_WIDTH - GATE_RANK), F32)],
        axis=2)
    wfu = jnp.concatenate([w_forget_up, jnp.zeros((depth, LANE - GATE_RANK, GLA_KW), F32)], axis=1).astype(BF16)
    wpool = w_pool.astype(BF16)
    wr = jnp.concatenate(
        [router_group_w, router_expert_w.transpose(0, 2, 1, 3).reshape(depth, d, N_EXPERTS),
         jnp.zeros((depth, d, ROUTE_LANES - N_GROUPS - N_EXPERTS), F32)], axis=2)
    br = jnp.concatenate(
        [router_group_b, router_expert_b.reshape(depth, N_EXPERTS),
         jnp.zeros((depth, ROUTE_LANES - N_GROUPS - N_EXPERTS), F32)], axis=1)
    w1 = w_exp_gate.reshape(depth * N_EXPERTS, d, EXPERT_HIDDEN)
    w3 = w_exp_up.reshape(depth * N_EXPERTS, d, EXPERT_HIDDEN)
    w2 = w_exp_down.reshape(depth * N_EXPERTS, EXPERT_HIDDEN, d)
    hist0 = jnp.concatenate([jnp.zeros((depth, n_sb, 1, POOL_WIDTH), F32), cache_pool], axis=2)

    x, xb = _ln_in(x_prompt.reshape(n_p, d), x_sample.reshape(n_s, d), row2(ln_in_g), row2(ln_in_b))
    states_p, hists_p, states_s, hists_s = [], [], [], []
    for l in range(depth):
        qkvg = _in_proj_qkvg(xb, w_in, l)
        pa = _in_proj_pa(xb, w_pa, l)
        mix_w = (wfu[l], row2(b_forget[l]), row2(gla_norm_g[l]), wpool[l], row2(pool_scale[l]))
        mixed, sp, hp = _mixer(qkvg, pa, *mix_w, row0=0, n_seq=n_pb, seq_len=seq, n_hist=0)
        mixed, ss, hs = _mixer(qkvg, pa, *mix_w, row0=n_p, n_seq=n_sb, seq_len=dseq, n_hist=POOL_HIST,
                               init=(state_gla[l], hist0[l]), mixed_in=mixed)
        x1, ri, rw = _out_proj(mixed, x, w_out, row2(ln1_g[l]), row2(ln1_b[l]), wr[l], row2(br[l]), l)
        pos, *plan = _dispatch_plan(ri[:, :2])
        xs = _dispatch(pos, x1)
        ys = _experts(plan, xs, w1, w3, w2, l)
        if l + 1 < depth:
            x, xb = _combine(pos, x1, rw, ys, row2(ln2_g[l]), row2(ln2_b[l]))
        else:
            y_p, y_s = _combine(pos, x1, rw, ys, row2(ln2_g[l]), row2(ln2_b[l]), split=(n_p, n_s))
        states_p.append(sp)
        hists_p.append(hp[:, 1:])
        states_s.append(ss)
        hists_s.append(hs[:, 1:])
    return (y_p.reshape(n_pb, seq, d), y_s.reshape(n_sb, dseq, d),
            jnp.stack(states_p), jnp.stack(hists_p), jnp.stack(states_s), jnp.stack(hists_s))
```

```python
import functools

import jax
import jax.numpy as jnp
from jax import lax
from jax.experimental import pallas as pl
from jax.experimental.pallas import tpu as pltpu

D_MODEL = 2048
DEPTH = 4
CHUNK = 64

GLA_HEADS = 4
GLA_WIDTH = D_MODEL // 2
GLA_DV = GLA_WIDTH // GLA_HEADS
GLA_DK = GLA_DV // 2
GLA_KW = GLA_HEADS * GLA_DK
GATE_RANK = 16
GATE_TAU = 16.0

POOL_WINDOWS = (2, 4, 8, 16)
POOL_GROUPS = len(POOL_WINDOWS)
POOL_WIDTH = D_MODEL - GLA_WIDTH
POOL_GC = POOL_WIDTH // POOL_GROUPS
POOL_HIST = max(POOL_WINDOWS) - 1
HIST_ROWS = POOL_HIST + 1

Q_OFF = 0
K_OFF = Q_OFF + GLA_KW
V_OFF = K_OFF + GLA_KW
G_OFF = V_OFF + GLA_WIDTH
A_OFF = G_OFF + GLA_WIDTH
P_OFF = A_OFF + GATE_RANK
IN_COLS = P_OFF + POOL_WIDTH
PA_COLS = POOL_WIDTH + 128

N_GROUPS = 4
N_EXP = 8
N_EXPERTS = N_GROUPS * N_EXP
EXPERT_HIDDEN = D_MODEL // 4
ROUTE_LANES = 128
ROUTER_CORRECTION = 0.5

DN_ALPHA = (2 * DEPTH) ** 0.25
LN_EPS = 1e-5
RMS_EPS = 1e-6

VMEM_LIMIT_BYTES = 56 * 1024 * 1024
LANE = 128
EXPERT_TILE = 256
BF16 = jnp.bfloat16
F32 = jnp.float32


def _params(sem, vmem=VMEM_LIMIT_BYTES):
    return pltpu.CompilerParams(dimension_semantics=sem, vmem_limit_bytes=vmem)


def _row_tile(n, prefs=(640, 512, 320, 256, 128)):
    for t in prefs:
        if n % t == 0:
            return t
    raise ValueError(f"no row tile for {n} rows")


def _layer_norm(x, g, b):
    mu = jnp.mean(x, axis=-1, keepdims=True)
    xc = x - mu
    var = jnp.mean(xc * xc, axis=-1, keepdims=True)
    return xc * lax.rsqrt(var + LN_EPS) * g + b


def _dot(a, b):
    return jnp.dot(a, b, preferred_element_type=F32)


def _dot_nt(a, b):
    return lax.dot_general(a, b, (((1,), (1,)), ((), ())), preferred_element_type=F32)


def _dot_split(m_bf16, x):
    hi = x.astype(BF16)
    lo = (x - hi.astype(F32)).astype(BF16)
    return _dot(m_bf16, hi) + _dot(m_bf16, lo)


def _ln_in_kernel(xp_ref, xs_ref, g_ref, b_ref, x_ref, xb_ref, *, n_prompt_blocks):
    i = pl.program_id(0)
    x = jnp.where(i < n_prompt_blocks, xp_ref[...], xs_ref[...])
    y = _layer_norm(x, g_ref[...], b_ref[...])
    x_ref[...] = y
    xb_ref[...] = y.astype(BF16)


def _ln_in(xp, xs, g, b):
    n_p, n_s = xp.shape[0], xs.shape[0]
    tm = n_s
    assert n_p % tm == 0
    npb = n_p // tm
    n = n_p + n_s
    return pl.pallas_call(
        functools.partial(_ln_in_kernel, n_prompt_blocks=npb),
        grid=(npb + 1,),
        in_specs=[
            pl.BlockSpec((tm, D_MODEL), lambda i: (jnp.minimum(i, npb - 1), 0)),
            pl.BlockSpec((tm, D_MODEL), lambda i: (0, 0)),
            pl.BlockSpec((1, D_MODEL), lambda i: (0, 0)),
            pl.BlockSpec((1, D_MODEL), lambda i: (0, 0)),
        ],
        out_specs=[
            pl.BlockSpec((tm, D_MODEL), lambda i: (i, 0)),
            pl.BlockSpec((tm, D_MODEL), lambda i: (i, 0)),
        ],
        out_shape=[jax.ShapeDtypeStruct((n, D_MODEL), F32), jax.ShapeDtypeStruct((n, D_MODEL), BF16)],
        compiler_params=_params(("arbitrary",)),
        name="ln_in",
    )(xp, xs, g, b)


def _proj_kernel(x_ref, w_ref, o_ref, wb_ref):
    @pl.when(pl.program_id(1) == 0)
    def _():
        wb_ref[...] = w_ref[...].astype(BF16)

    o_ref[...] = _dot(x_ref[...], wb_ref[...]).astype(o_ref.dtype)


def _in_proj_qkvg(xb, w_in, layer):
    n = xb.shape[0]
    tm, tn = _row_tile(n), 1024
    return pl.pallas_call(
        _proj_kernel,
        grid=(A_OFF // tn, n // tm),
        in_specs=[
            pl.BlockSpec((tm, D_MODEL), lambda j, m: (m, 0)),
            pl.BlockSpec((None, D_MODEL, tn), lambda j, m: (layer, 0, j)),
        ],
        out_specs=pl.BlockSpec((tm, tn), lambda j, m: (m, j)),
        out_shape=jax.ShapeDtypeStruct((n, A_OFF), BF16),
        scratch_shapes=[pltpu.VMEM((D_MODEL, tn), BF16)],
        compiler_params=_params(("arbitrary", "arbitrary")),
        name="in_proj_qkvg",
    )(xb, w_in)


def _in_proj_pa(xb, w_pa, layer):
    n = xb.shape[0]
    tm = _row_tile(n)
    return pl.pallas_call(
        _proj_kernel,
        grid=(1, n // tm),
        in_specs=[
            pl.BlockSpec((tm, D_MODEL), lambda j, m: (m, 0)),
            pl.BlockSpec((None, D_MODEL, PA_COLS), lambda j, m: (layer, 0, 0)),
        ],
        out_specs=pl.BlockSpec((tm, PA_COLS), lambda j, m: (m, 0)),
        out_shape=jax.ShapeDtypeStruct((n, PA_COLS), F32),
        scratch_shapes=[pltpu.VMEM((D_MODEL, PA_COLS), BF16)],
        compiler_params=_params(("arbitrary", "arbitrary")),
        name="in_proj_pa",
    )(xb, w_pa)


def _mixer_kernel(*refs, n_seq, n_blk, n_chunks, valid, n_hist, has_init, fill_tail):
    if has_init:
        (qkvg_ref, pa_ref, wfu_ref, bf_ref, gain_ref, wpool_ref, pscale_ref, s0_ref, h0_ref, _mixed_in,
         mixed_ref, sout_ref, hout_ref, s_scr, z_scr) = refs
    else:
        (qkvg_ref, pa_ref, wfu_ref, bf_ref, gain_ref, wpool_ref, pscale_ref,
         mixed_ref, sout_ref, hout_ref, s_scr, z_scr) = refs
    C = CHUNK
    step = pl.program_id(0)
    blk = step % n_blk

    def pad_rows(x):
        if valid == C:
            return x
        return jnp.concatenate([x, jnp.zeros((C - valid, x.shape[1]), x.dtype)], axis=0)

    def process():
        @pl.when(blk == 0)
        def _():
            if has_init:
                s_scr[...] = s0_ref[...]
                z_scr[0:C - HIST_ROWS, :] = jnp.zeros((C - HIST_ROWS, POOL_WIDTH), F32)
                z_scr[C - HIST_ROWS:C, :] = h0_ref[...]
            else:
                s_scr[...] = jnp.zeros(s_scr.shape, F32)
                z_scr[...] = jnp.zeros(z_scr.shape, F32)

        row = lax.broadcasted_iota(jnp.int32, (C, C), 0)
        col = lax.broadcasted_iota(jnp.int32, (C, C), 1)
        causal = row >= col
        tri = causal.astype(BF16)
        t_idx = lax.broadcasted_iota(jnp.int32, (C, 2 * C), 0)
        s_idx = lax.broadcasted_iota(jnp.int32, (C, 2 * C), 1) - C
        lag = t_idx - s_idx
        bands = [((lag >= 0) & (lag < w)).astype(BF16) for w in POOL_WINDOWS]

        def chunk(j, carry):
            if n_chunks == 1:
                rows = slice(0, valid)
            else:
                rows = pl.ds(pl.multiple_of(j * C, C), C)
            qkvg = pad_rows(qkvg_ref[rows, :])
            pa = pad_rows(pa_ref[rows, :])
            q = qkvg[:, Q_OFF:K_OFF].astype(F32)
            k = qkvg[:, K_OFF:V_OFF].astype(F32)
            v = qkvg[:, V_OFF:G_OFF]
            g = qkvg[:, G_OFF:A_OFF].astype(F32)
            zc = pa[:, :POOL_WIDTH]
            a = pa[:, POOL_WIDTH:]

            la = jax.nn.log_sigmoid(_dot(a.astype(BF16), wfu_ref[...]) + bf_ref[...]) * (1.0 / GATE_TAU)
            if valid != C:
                la = jnp.where(lax.broadcasted_iota(jnp.int32, la.shape, 0) < valid, la, 0.0)
            bcum = _dot_split(tri, la)
            b_last = bcum[C - 1:C, :]
            q_t = (q * jnp.exp(bcum) * (GLA_DK ** -0.5)).astype(BF16)
            k_t = (k * jnp.exp(-bcum)).astype(BF16)
            k_dec = k * jnp.exp(b_last - bcum)
            e_last = jnp.exp(b_last)
            outs = []
            for h in range(GLA_HEADS):
                ks = slice(h * GLA_DK, (h + 1) * GLA_DK)
                vs = slice(h * GLA_DV, (h + 1) * GLA_DV)
                s_h = s_scr[h]
                att = jnp.where(causal, _dot_nt(q_t[:, ks], k_t[:, ks]), 0.0)
                o = _dot(att.astype(BF16), v[:, vs]) + _dot(q_t[:, ks], s_h.astype(BF16))
                k_aug = jnp.concatenate([k_dec[:, ks], jnp.zeros((GLA_DK - C, GLA_DK), F32)], axis=0)
                v_aug = jnp.concatenate([v[:, vs], jnp.zeros((GLA_DK - C, GLA_DV), BF16)], axis=0)
                decay = jnp.broadcast_to(e_last[:, ks], (GLA_DK, GLA_DK)).T
                decay = jnp.concatenate([decay] * (GLA_DV // GLA_DK), axis=1)
                s_scr[h] = decay * s_h + _dot(k_aug.T.astype(BF16), v_aug)
                outs.append(o * lax.rsqrt(jnp.mean(o * o, axis=-1, keepdims=True) + RMS_EPS))
            gla = jnp.concatenate(outs, axis=1) * gain_ref[...] * (g * jax.nn.sigmoid(g))

            z_ext = jnp.concatenate([z_scr[...], zc], axis=0)
            t0 = (blk * n_chunks + j) * C + n_hist + 1
            avail = lax.broadcasted_iota(jnp.int32, (C, POOL_GC), 0) + t0
            pooled = []
            for gi, w in enumerate(POOL_WINDOWS):
                cs = slice(gi * POOL_GC, (gi + 1) * POOL_GC)
                cnt = jnp.minimum(avail, w).astype(F32)
                m = _dot_split(bands[gi], z_ext[:, cs]) / cnt - zc[:, cs]
                pooled.append(_dot(m.astype(BF16), wpool_ref[gi]))
            pool = jnp.concatenate(pooled, axis=1) * pscale_ref[...]
            z_scr[...] = zc

            res = jnp.concatenate([gla, pool], axis=1).astype(BF16)
            mixed_ref[rows, :] = res[0:valid]
            return carry

        if n_chunks == 1:
            chunk(0, 0)
        else:
            lax.fori_loop(0, n_chunks, chunk, 0, unroll=2)

        @pl.when(blk == n_blk - 1)
        def _():
            sout_ref[...] = s_scr[...]
            hout_ref[...] = z_scr[valid - HIST_ROWS:valid, :]

    if fill_tail:
        pl.when(step < n_seq * n_blk)(process)

        @pl.when(step == n_seq * n_blk)
        def _():
            mixed_ref[...] = jnp.zeros(mixed_ref.shape, BF16)
    else:
        process()


def _mixer(qkvg, pa, wfu, bfg, gain, wpool, pscale, *, row0, n_seq, seq_len, n_hist, init=None, mixed_in=None):
    n = qkvg.shape[0]
    if seq_len % CHUNK == 0:
        rb, valid = _row_tile(seq_len, (512, 256, 128, 64)), CHUNK
    else:
        rb, valid = seq_len, seq_len
    assert seq_len % rb == 0 and row0 % rb == 0 and valid >= HIST_ROWS
    n_blk = seq_len // rb
    n_chunks = rb // valid
    base = row0 // rb
    n_steps = n_seq * n_blk
    has_init = init is not None
    assert has_init == (mixed_in is not None)
    tail_rows = n - (row0 + n_seq * seq_len)
    fill_tail = (not has_init) and tail_rows > 0
    assert tail_rows <= rb

    rows = lambda t: (base + t, 0)
    const2 = lambda t: (0, 0)
    seq3 = lambda t: (jnp.minimum(t // n_blk, n_seq - 1), 0, 0)
    seq4 = lambda t: (jnp.minimum(t // n_blk, n_seq - 1), 0, 0, 0)

    in_specs = [
        pl.BlockSpec((rb, A_OFF), rows),
        pl.BlockSpec((rb, PA_COLS), rows),
        pl.BlockSpec((LANE, GLA_KW), const2),
        pl.BlockSpec((1, GLA_KW), const2),
        pl.BlockSpec((1, GLA_WIDTH), const2),
        pl.BlockSpec((POOL_GROUPS, POOL_GC, POOL_GC), lambda t: (0, 0, 0)),
        pl.BlockSpec((1, POOL_WIDTH), const2),
    ]
    args = [qkvg, pa, wfu, bfg, gain, wpool, pscale]
    aliases = {}
    if has_init:
        in_specs += [
            pl.BlockSpec((None, GLA_HEADS, GLA_DK, GLA_DV), seq4),
            pl.BlockSpec((None, HIST_ROWS, POOL_WIDTH), seq3),
            pl.BlockSpec(memory_space=pl.ANY),
        ]
        args += [init[0], init[1], mixed_in]
        aliases = {len(args) - 1: 0}
    return pl.pallas_call(
        functools.partial(_mixer_kernel, n_seq=n_seq, n_blk=n_blk, n_chunks=n_chunks, valid=valid, n_hist=n_hist,
                          has_init=has_init, fill_tail=fill_tail),
        grid=(n_steps + int(fill_tail),),
        in_specs=in_specs,
        out_specs=[
            pl.BlockSpec((rb, D_MODEL), rows),
            pl.BlockSpec((None, GLA_HEADS, GLA_DK, GLA_DV), seq4),
            pl.BlockSpec((None, HIST_ROWS, POOL_WIDTH), seq3),
        ],
        out_shape=[
            jax.ShapeDtypeStruct((n, D_MODEL), BF16),
            jax.ShapeDtypeStruct((n_seq, GLA_HEADS, GLA_DK, GLA_DV), F32),
            jax.ShapeDtypeStruct((n_seq, HIST_ROWS, POOL_WIDTH), F32),
        ],
        scratch_shapes=[
            pltpu.VMEM((GLA_HEADS, GLA_DK, GLA_DV), F32),
            pltpu.VMEM((CHUNK, POOL_WIDTH), F32),
        ],
        input_output_aliases=aliases,
        compiler_params=_params(("arbitrary",)),
        name="mixer_init" if has_init else "mixer",
    )(*args)


def _route(logits):
    lane = lax.broadcasted_iota(jnp.int32, logits.shape, 1).astype(F32)
    neg = -jnp.inf
    first_lane = lambda hit: jnp.min(jnp.where(hit, lane, float(ROUTE_LANES)), axis=-1, keepdims=True)
    lg = jnp.where(lane < N_GROUPS, logits, neg)
    un = jnp.exp(lg - jnp.max(lg, axis=-1, keepdims=True))
    pg = un / jnp.sum(un, axis=-1, keepdims=True)
    top_pg = jnp.max(pg, axis=-1, keepdims=True)
    gsel = first_lane((pg == top_pg) & (lane < N_GROUPS))
    first = N_GROUPS + gsel * N_EXP
    le = jnp.where((lane >= first) & (lane < first + N_EXP), logits, neg)
    v0 = jnp.max(le, axis=-1, keepdims=True)
    i0 = first_lane(le == v0)
    le = jnp.where(lane == i0, neg, le)
    v1 = jnp.max(le, axis=-1, keepdims=True)
    i1 = first_lane(le == v1)
    u1 = jnp.exp(v1 - v0)
    den = 1.0 + u1
    e0 = (i0 - N_GROUPS).astype(jnp.int32)
    e1 = (i1 - N_GROUPS).astype(jnp.int32)
    return e0, e1, (1.0 / den) * top_pg, (u1 / den) * top_pg


def _out_proj_kernel(mixed_ref, x_ref, w_ref, g_ref, b_ref, wr_ref, br_ref, x1_ref, ri_ref, rw_ref, wrs_ref):
    tm = x_ref.shape[0]

    @pl.when(pl.program_id(0) == 0)
    def _():
        wr = wr_ref[...]
        wr_hi = wr.astype(BF16)
        wrs_ref[:, :ROUTE_LANES] = wr_hi
        wrs_ref[:, ROUTE_LANES:] = (wr - wr_hi.astype(F32)).astype(BF16)

    y = _dot(mixed_ref[...], w_ref[...])
    x1 = _layer_norm(DN_ALPHA * x_ref[...] + y, g_ref[...], b_ref[...])
    x1_ref[...] = x1
    x_hi = x1.astype(BF16)
    x_lo = (x1 - x_hi.astype(F32)).astype(BF16)
    prod = _dot(jnp.concatenate([x_hi, x_lo], axis=0), wrs_ref[...])
    corr = prod[tm:, :ROUTE_LANES] + prod[:tm, ROUTE_LANES:]
    logits = prod[:tm, :ROUTE_LANES] + ROUTER_CORRECTION * corr + br_ref[...]
    e0, e1, w0, w1 = _route(logits)
    lane = lax.broadcasted_iota(jnp.int32, logits.shape, 1)
    ri_ref[...] = jnp.where(lane == 0, e0, jnp.where(lane == 1, e1, 0))
    rw_ref[...] = jnp.where(lane == 0, w0, jnp.where(lane == 1, w1, 0.0))


def _out_proj(mixed, x, w_out, g, b, wr, br, layer):
    n = x.shape[0]
    tm = _row_tile(n, (640, 416, 320, 256, 128))
    row = lambda i: (i, 0)
    const = lambda i: (0, 0)
    return pl.pallas_call(
        _out_proj_kernel,
        grid=(n // tm,),
        in_specs=[
            pl.BlockSpec((tm, D_MODEL), row),
            pl.BlockSpec((tm, D_MODEL), row),
            pl.BlockSpec((None, D_MODEL, D_MODEL), lambda i: (layer, 0, 0), pipeline_mode=pl.Buffered(1)),
            pl.BlockSpec((1, D_MODEL), const),
            pl.BlockSpec((1, D_MODEL), const),
            pl.BlockSpec((D_MODEL, ROUTE_LANES), const),
            pl.BlockSpec((1, ROUTE_LANES), const),
        ],
        out_specs=[
            pl.BlockSpec((tm, D_MODEL), row),
            pl.BlockSpec((tm, ROUTE_LANES), row),
            pl.BlockSpec((tm, ROUTE_LANES), row),
        ],
        out_shape=[
            jax.ShapeDtypeStruct((n, D_MODEL), F32),
            jax.ShapeDtypeStruct((n, ROUTE_LANES), jnp.int32),
            jax.ShapeDtypeStruct((n, ROUTE_LANES), F32),
        ],
        scratch_shapes=[pltpu.VMEM((D_MODEL, 2 * ROUTE_LANES), BF16)],
        compiler_params=_params(("arbitrary",)),
        name="out_proj_ln_route",
    )(mixed, x, w_out, g, b, wr, br)


def _dispatch_plan(eid):
    n_pairs = eid.shape[0] * 2
    n_tiles = n_pairs // EXPERT_TILE
    flat = eid.reshape(-1)
    onehot = (flat[:, None] == jnp.arange(N_EXPERTS, dtype=jnp.int32)[None, :]).astype(jnp.int32)
    csum = jnp.cumsum(onehot, axis=0)
    rank = jnp.sum(csum * onehot, axis=1) - 1
    counts = csum[-1]
    offs = jnp.concatenate([jnp.zeros((1,), jnp.int32), jnp.cumsum(counts)])
    pos = (offs[flat] + rank).astype(jnp.int32)
    bounds = jnp.sort(jnp.concatenate([jnp.arange(n_tiles, dtype=jnp.int32) * EXPERT_TILE, offs[:N_EXPERTS]]))
    seg_lo = bounds
    seg_hi = jnp.concatenate([bounds[1:], jnp.full((1,), n_pairs, jnp.int32)])
    tile = jnp.minimum(seg_lo // EXPERT_TILE, n_tiles - 1)
    expert = jnp.minimum(jnp.searchsorted(offs[1:], seg_lo, side="right"), N_EXPERTS - 1).astype(jnp.int32)
    lo = seg_lo - tile * EXPERT_TILE
    hi = jnp.where(seg_hi > seg_lo, seg_hi - tile * EXPERT_TILE, lo)
    valid = seg_hi > seg_lo
    e_seen = lax.cummax(jnp.where(valid, expert, -1))
    prev = jnp.concatenate([jnp.full((1,), -1, jnp.int32), e_seen[:-1]])
    fetch = valid & (expert > prev)
    run = jnp.maximum(jnp.cumsum(fetch.astype(jnp.int32)) - 1, 0)
    slot = run % 2
    fetch = fetch.astype(jnp.int32) * jnp.where(run == 0, 2, 1)
    ids = jnp.arange(N_EXPERTS, dtype=jnp.int32)
    later = lax.cummin(jnp.where(counts > 0, ids, N_EXPERTS), reverse=True)
    nxt_of = jnp.concatenate([later[1:], jnp.full((1,), N_EXPERTS, jnp.int32)])
    nxt = jnp.where(nxt_of[expert] < N_EXPERTS, nxt_of[expert], -1)
    sched = (fetch, slot.astype(jnp.int32), nxt.astype(jnp.int32))
    return pos, (tile.astype(jnp.int32), expert, lo.astype(jnp.int32), hi.astype(jnp.int32)) + sched


def _dispatch_kernel(pos_ref, x_ref, xs_hbm, sem, *, tb):
    base = pl.program_id(0) * tb

    def scatter(t, k):
        return pltpu.make_async_copy(x_ref.at[pl.ds(t, 1)], xs_hbm.at[pl.ds(pos_ref[2 * (base + t) + k], 1)], sem)

    def issue(t, c):
        scatter(t, 0).start()
        scatter(t, 1).start()
        return c

    lax.fori_loop(0, tb, issue, 0, unroll=8)

    def drain(t, c):
        scatter(t, 0).wait()
        scatter(t, 1).wait()
        return c

    lax.fori_loop(0, tb, drain, 0, unroll=8)


def _dispatch(pos, x1):
    n = x1.shape[0]
    tb = _row_tile(n)
    return pl.pallas_call(
        functools.partial(_dispatch_kernel, tb=tb),
        grid_spec=pltpu.PrefetchScalarGridSpec(
            num_scalar_prefetch=1,
            grid=(n // tb,),
            in_specs=[pl.BlockSpec((tb, D_MODEL), lambda i, p: (i, 0))],
            out_specs=pl.BlockSpec(memory_space=pl.ANY),
            scratch_shapes=[pltpu.SemaphoreType.DMA(())],
        ),
        out_shape=jax.ShapeDtypeStruct((2 * n, D_MODEL), F32),
        compiler_params=_params(("arbitrary",)),
        name="moe_dispatch",
    )(pos, x1)


def _experts_kernel(tile_ref, exp_ref, lo_ref, hi_ref, fetch_ref, slot_ref, nxt_ref,
                    xs_ref, w1_hbm, w3_hbm, w2_hbm, o_ref, w1_buf, w3_buf, w2_buf, sem, *, layer):
    i = pl.program_id(0)
    lo, hi = lo_ref[i], hi_ref[i]
    slot = slot_ref[i]

    def weight_copies(expert, s):
        e = layer * N_EXPERTS + expert
        return (pltpu.make_async_copy(w1_hbm.at[e], w1_buf.at[s], sem.at[s, 0]),
                pltpu.make_async_copy(w3_hbm.at[e], w3_buf.at[s], sem.at[s, 1]),
                pltpu.make_async_copy(w2_hbm.at[e], w2_buf.at[s], sem.at[s, 2]))

    @pl.when(fetch_ref[i] == 2)
    def _():
        for c in weight_copies(exp_ref[i], slot):
            c.start()

    @pl.when(fetch_ref[i] > 0)
    def _():
        for c in weight_copies(exp_ref[i], slot):
            c.wait()

        @pl.when(nxt_ref[i] >= 0)
        def _():
            for c in weight_copies(nxt_ref[i], 1 - slot):
                c.start()

    def compute():
        x = xs_ref[...].astype(BF16)
        h1 = _dot(x, w1_buf[slot].astype(BF16))
        h3 = _dot(x, w3_buf[slot].astype(BF16))
        a = (h1 * jax.nn.sigmoid(h1) * h3).astype(BF16)
        return _dot(a, w2_buf[slot].astype(BF16))

    @pl.when((hi > lo) & (lo == 0))
    def _():
        o_ref[...] = compute()

    @pl.when((hi > lo) & (lo > 0))
    def _():
        r = lax.broadcasted_iota(jnp.int32, o_ref.shape, 0)
        o_ref[...] = jnp.where((r >= lo) & (r < hi), compute(), o_ref[...])


def _experts(plan, xs, w1, w3, w2, layer):
    n_items = plan[0].shape[0]
    xrow = lambda i, t, *_: (t[i], 0)
    any_spec = pl.BlockSpec(memory_space=pl.ANY)
    return pl.pallas_call(
        functools.partial(_experts_kernel, layer=layer),
        grid_spec=pltpu.PrefetchScalarGridSpec(
            num_scalar_prefetch=len(plan),
            grid=(n_items,),
            in_specs=[pl.BlockSpec((EXPERT_TILE, D_MODEL), xrow), any_spec, any_spec, any_spec],
            out_specs=pl.BlockSpec((EXPERT_TILE, D_MODEL), xrow),
            scratch_shapes=[
                pltpu.VMEM((2, D_MODEL, EXPERT_HIDDEN), F32),
                pltpu.VMEM((2, D_MODEL, EXPERT_HIDDEN), F32),
                pltpu.VMEM((2, EXPERT_HIDDEN, D_MODEL), F32),
                pltpu.SemaphoreType.DMA((2, 3)),
            ],
        ),
        out_shape=jax.ShapeDtypeStruct(xs.shape, F32),
        compiler_params=_params(("arbitrary",)),
        name="moe_experts",
    )(*plan, xs, w1, w3, w2)


def _combine_kernel(pos_ref, x1_ref, rw_ref, ys_hbm, g_ref, b_ref, *rest, tm, n_prompt_blocks):
    if n_prompt_blocks is None:
        x2_ref, x2b_ref, buf, sem = rest
    else:
        yp_ref, ysm_ref, buf, sem = rest
    i = pl.program_id(0)
    base = i * tm

    def gather(t, k):
        src = ys_hbm.at[pl.ds(pos_ref[2 * (base + t) + k], 1)]
        return pltpu.make_async_copy(src, buf.at[k, pl.ds(t, 1)], sem)

    def issue(t, c):
        gather(t, 0).start()
        gather(t, 1).start()
        return c

    lax.fori_loop(0, tm, issue, 0, unroll=8)

    def drain(t, c):
        gather(t, 0).wait()
        gather(t, 1).wait()
        return c

    lax.fori_loop(0, tm, drain, 0, unroll=8)

    rw = rw_ref[...]
    f = rw[:, 0:1] * buf[0] + rw[:, 1:2] * buf[1]
    x2 = _layer_norm(DN_ALPHA * x1_ref[...] + f, g_ref[...], b_ref[...])
    if n_prompt_blocks is None:
        x2_ref[...] = x2
        x2b_ref[...] = x2.astype(BF16)
    else:
        @pl.when(i < n_prompt_blocks)
        def _():
            yp_ref[...] = x2

        @pl.when(i >= n_prompt_blocks)
        def _():
            ysm_ref[...] = x2


def _combine(pos, x1, rw, ys, g, b, *, split=None):
    n = x1.shape[0]
    row = lambda i, p: (i, 0)
    const = lambda i, p: (0, 0)
    if split is None:
        tm, npb = _row_tile(n, (320, 256, 128)), None
        out_specs = [pl.BlockSpec((tm, D_MODEL), row), pl.BlockSpec((tm, D_MODEL), row)]
        out_shape = [jax.ShapeDtypeStruct((n, D_MODEL), F32), jax.ShapeDtypeStruct((n, D_MODEL), BF16)]
    else:
        n_p, n_s = split
        tm = n_s
        assert n_p % tm == 0 and n_p + n_s == n
        npb = n_p // tm
        out_specs = [
            pl.BlockSpec((tm, D_MODEL), lambda i, p: (jnp.minimum(i, npb - 1), 0)),
            pl.BlockSpec((tm, D_MODEL), lambda i, p: (jnp.maximum(i - npb, 0), 0)),
        ]
        out_shape = [jax.ShapeDtypeStruct((n_p, D_MODEL), F32), jax.ShapeDtypeStruct((n_s, D_MODEL), F32)]
    return pl.pallas_call(
        functools.partial(_combine_kernel, tm=tm, n_prompt_blocks=npb),
        grid_spec=pltpu.PrefetchScalarGridSpec(
            num_scalar_prefetch=1,
            grid=(n // tm,),
            in_specs=[
                pl.BlockSpec((tm, D_MODEL), row),
                pl.BlockSpec((tm, ROUTE_LANES), row),
                pl.BlockSpec(memory_space=pl.ANY),
                pl.BlockSpec((1, D_MODEL), const),
                pl.BlockSpec((1, D_MODEL), const),
            ],
            out_specs=out_specs,
            scratch_shapes=[pltpu.VMEM((2, tm, D_MODEL), F32), pltpu.SemaphoreType.DMA(())],
        ),
        out_shape=out_shape,
        compiler_params=_params(("arbitrary",)),
        name="moe_combine_ln" if split is None else "moe_combine_ln_final",
    )(pos, x1, rw, ys, g, b)


def kernel(x_prompt, x_sample, state_gla, cache_pool, ln_in_g, ln_in_b, w_in, w_forget_up, b_forget, gla_norm_g, w_pool, pool_scale, w_out, ln1_g, ln1_b, router_group_w, router_group_b, router_expert_w, router_expert_b, w_exp_gate, w_exp_up, w_exp_down, ln2_g, ln2_b):
    n_pb, seq, d = x_prompt.shape
    n_sb, dseq, _ = x_sample.shape
    n_p, n_s = n_pb * seq, n_sb * dseq
    depth = w_in.shape[0]
    row2 = lambda v: v.reshape(1, -1)

    w_out_b = w_out.astype(BF16)
    w_pa = jnp.concatenate(
        [w_in[:, :, P_OFF:], w_in[:, :, A_OFF:P_OFF], jnp.zeros((depth, d, PA_COLS - POOL_WIDTH - GATE_RANK), F32)],
        axis=2)
    wfu = jnp.concatenate([w_forget_up, jnp.zeros((depth, LANE - GATE_RANK, GLA_KW), F32)], axis=1).astype(BF16)
    wpool = w_pool.astype(BF16)
    wr = jnp.concatenate(
        [router_group_w, router_expert_w.transpose(0, 2, 1, 3).reshape(depth, d, N_EXPERTS),
         jnp.zeros((depth, d, ROUTE_LANES - N_GROUPS - N_EXPERTS), F32)], axis=2)
    br = jnp.concatenate(
        [router_group_b, router_expert_b.reshape(depth, N_EXPERTS),
         jnp.zeros((depth, ROUTE_LANES - N_GROUPS - N_EXPERTS), F32)], axis=1)
    w1 = w_exp_gate.reshape(depth * N_EXPERTS, d, EXPERT_HIDDEN)
    w3 = w_exp_up.reshape(depth * N_EXPERTS, d, EXPERT_HIDDEN)
    w2 = w_exp_down.reshape(depth * N_EXPERTS, EXPERT_HIDDEN, d)
    hist0 = jnp.concatenate([jnp.zeros((depth, n_sb, 1, POOL_WIDTH), F32), cache_pool], axis=2)

    x, xb = _ln_in(x_prompt.reshape(n_p, d), x_sample.reshape(n_s, d), row2(ln_in_g), row2(ln_in_b))
    states_p, hists_p, states_s, hists_s = [], [], [], []
    for l in range(depth):
        qkvg = _in_proj_qkvg(xb, w_in, l)
        pa = _in_proj_pa(xb, w_pa, l)
        mix_w = (wfu[l], row2(b_forget[l]), row2(gla_norm_g[l]), wpool[l], row2(pool_scale[l]))
        mixed, sp, hp = _mixer(qkvg, pa, *mix_w, row0=0, n_seq=n_pb, seq_len=seq, n_hist=0)
        mixed, ss, hs = _mixer(qkvg, pa, *mix_w, row0=n_p, n_seq=n_sb, seq_len=dseq, n_hist=POOL_HIST,
                               init=(state_gla[l], hist0[l]), mixed_in=mixed)
        x1, ri, rw = _out_proj(mixed, x, w_out_b, row2(ln1_g[l]), row2(ln1_b[l]), wr[l], row2(br[l]), l)
        pos, plan = _dispatch_plan(ri[:, :2])
        xs = _dispatch(pos, x1)
        ys = _experts(plan, xs, w1, w3, w2, l)
        if l + 1 < depth:
            x, xb = _combine(pos, x1, rw, ys, row2(ln2_g[l]), row2(ln2_b[l]))
        else:
            y_p, y_s = _combine(pos, x1, rw, ys, row2(ln2_g[l]), row2(ln2_b[l]), split=(n_p, n_s))
        states_p.append(sp)
        hists_p.append(hp[:, 1:])
        states_s.append(ss)
        hists_s.append(hs[:, 1:])
    return (y_p.reshape(n_pb, seq, d), y_s.reshape(n_sb, dseq, d),
            jnp.stack(states_p), jnp.stack(hists_p), jnp.stack(states_s), jnp.stack(hists_s))
```

```python
import functools

import jax
import jax.numpy as jnp
from jax import lax
from jax.experimental import pallas as pl
from jax.experimental.pallas import tpu as pltpu

D_MODEL = 2048
DEPTH = 4
CHUNK = 64

GLA_HEADS = 4
GLA_WIDTH = D_MODEL // 2
GLA_DV = GLA_WIDTH // GLA_HEADS
GLA_DK = GLA_DV // 2
GLA_KW = GLA_HEADS * GLA_DK
GATE_RANK = 16
GATE_TAU = 16.0

POOL_WINDOWS = (2, 4, 8, 16)
POOL_GROUPS = len(POOL_WINDOWS)
POOL_WIDTH = D_MODEL - GLA_WIDTH
POOL_GC = POOL_WIDTH // POOL_GROUPS
POOL_HIST = max(POOL_WINDOWS) - 1
HIST_ROWS = POOL_HIST + 1

Q_OFF = 0
K_OFF = Q_OFF + GLA_KW
V_OFF = K_OFF + GLA_KW
G_OFF = V_OFF + GLA_WIDTH
A_OFF = G_OFF + GLA_WIDTH
P_OFF = A_OFF + GATE_RANK
IN_COLS = P_OFF + POOL_WIDTH
PA_COLS = POOL_WIDTH + 128

N_GROUPS = 4
N_EXP = 8
N_EXPERTS = N_GROUPS * N_EXP
EXPERT_HIDDEN = D_MODEL // 4
ROUTE_LANES = 128
ROUTER_CORRECTION = 0.5

DN_ALPHA = (2 * DEPTH) ** 0.25
LN_EPS = 1e-5
RMS_EPS = 1e-6

VMEM_LIMIT_BYTES = 56 * 1024 * 1024
LANE = 128
EXPERT_TILE = 256
BF16 = jnp.bfloat16
F32 = jnp.float32


def _params(sem, vmem=VMEM_LIMIT_BYTES):
    return pltpu.CompilerParams(dimension_semantics=sem, vmem_limit_bytes=vmem)


def _row_tile(n, prefs=(640, 512, 320, 256, 128)):
    for t in prefs:
        if n % t == 0:
            return t
    raise ValueError(f"no row tile for {n} rows")


def _layer_norm(x, g, b):
    mu = jnp.mean(x, axis=-1, keepdims=True)
    xc = x - mu
    var = jnp.mean(xc * xc, axis=-1, keepdims=True)
    return xc * lax.rsqrt(var + LN_EPS) * g + b


def _dot(a, b):
    return jnp.dot(a, b, preferred_element_type=F32)


def _dot_nt(a, b):
    return lax.dot_general(a, b, (((1,), (1,)), ((), ())), preferred_element_type=F32)


def _pack_pairs(x):
    k = x.shape[1] // 2
    hi = lax.bitcast_convert_type(x[:, :k].astype(BF16).astype(F32), jnp.uint32)
    lo = lax.bitcast_convert_type(x[:, k:].astype(BF16).astype(F32), jnp.uint32)
    return hi | (lo >> 16)


def _unpack_pairs(w):
    hi = lax.bitcast_convert_type(w & jnp.uint32(0xFFFF0000), F32)
    lo = lax.bitcast_convert_type(w << 16, F32)
    return jnp.concatenate([hi, lo], axis=1)


def _dot_split(m_bf16, x):
    hi = x.astype(BF16)
    lo = (x - hi.astype(F32)).astype(BF16)
    return _dot(m_bf16, hi) + _dot(m_bf16, lo)


def _ln_in_kernel(xp_ref, xs_ref, g_ref, b_ref, x_ref, xb_ref, *, n_prompt_blocks):
    i = pl.program_id(0)
    x = jnp.where(i < n_prompt_blocks, xp_ref[...], xs_ref[...])
    y = _layer_norm(x, g_ref[...], b_ref[...])
    x_ref[...] = y
    xb_ref[...] = y.astype(BF16)


def _ln_in(xp, xs, g, b):
    n_p, n_s = xp.shape[0], xs.shape[0]
    tm = n_s
    assert n_p % tm == 0
    npb = n_p // tm
    n = n_p + n_s
    return pl.pallas_call(
        functools.partial(_ln_in_kernel, n_prompt_blocks=npb),
        grid=(npb + 1,),
        in_specs=[
            pl.BlockSpec((tm, D_MODEL), lambda i: (jnp.minimum(i, npb - 1), 0)),
            pl.BlockSpec((tm, D_MODEL), lambda i: (0, 0)),
            pl.BlockSpec((1, D_MODEL), lambda i: (0, 0)),
            pl.BlockSpec((1, D_MODEL), lambda i: (0, 0)),
        ],
        out_specs=[
            pl.BlockSpec((tm, D_MODEL), lambda i: (i, 0)),
            pl.BlockSpec((tm, D_MODEL), lambda i: (i, 0)),
        ],
        out_shape=[jax.ShapeDtypeStruct((n, D_MODEL), F32), jax.ShapeDtypeStruct((n, D_MODEL), BF16)],
        compiler_params=_params(("arbitrary",)),
        name="ln_in",
    )(xp, xs, g, b)


def _proj_qkvg_kernel(x_ref, wt_ref, o_ref, wb_ref):
    @pl.when(pl.program_id(1) == 0)
    def _():
        wb_ref[...] = wt_ref[...].astype(BF16)

    o_ref[...] = _dot_nt(x_ref[...], wb_ref[...]).astype(o_ref.dtype)


def _in_proj_qkvg(xb, w_in_t, layer):
    n = xb.shape[0]
    tm, tn = _row_tile(n), 1024
    return pl.pallas_call(
        _proj_qkvg_kernel,
        grid=(A_OFF // tn, n // tm),
        in_specs=[
            pl.BlockSpec((tm, D_MODEL), lambda j, m: (m, 0)),
            pl.BlockSpec((None, tn, D_MODEL), lambda j, m: (layer, j, 0)),
        ],
        out_specs=pl.BlockSpec((tm, tn), lambda j, m: (m, j)),
        out_shape=jax.ShapeDtypeStruct((n, A_OFF), BF16),
        scratch_shapes=[pltpu.VMEM((tn, D_MODEL), BF16)],
        compiler_params=_params(("arbitrary", "arbitrary")),
        name="in_proj_qkvg",
    )(xb, w_in_t)


def _proj_pa_kernel(x_ref, wt_hbm, o_ref, stage_ref, wb_ref, sem, *, layer):
    n_tail = IN_COLS - A_OFF

    @pl.when(pl.program_id(0) == 0)
    def _():
        copies = (
            pltpu.make_async_copy(wt_hbm.at[layer, pl.ds(P_OFF, POOL_WIDTH)], stage_ref.at[pl.ds(0, POOL_WIDTH)],
                                  sem.at[0]),
            pltpu.make_async_copy(wt_hbm.at[layer, pl.ds(A_OFF, GATE_RANK)],
                                  stage_ref.at[pl.ds(POOL_WIDTH, GATE_RANK)], sem.at[1]),
        )
        for c in copies:
            c.start()
        for c in copies:
            c.wait()
        wb_ref[0:n_tail, :] = stage_ref[...].astype(BF16)
        wb_ref[n_tail:, :] = jnp.zeros((PA_COLS - n_tail, D_MODEL), BF16)

    o_ref[...] = _dot_nt(x_ref[...], wb_ref[...])


def _in_proj_pa(xb, w_in_t, layer):
    n = xb.shape[0]
    tm = _row_tile(n)
    return pl.pallas_call(
        functools.partial(_proj_pa_kernel, layer=layer),
        grid=(n // tm,),
        in_specs=[
            pl.BlockSpec((tm, D_MODEL), lambda m: (m, 0)),
            pl.BlockSpec(memory_space=pl.ANY),
        ],
        out_specs=pl.BlockSpec((tm, PA_COLS), lambda m: (m, 0)),
        out_shape=jax.ShapeDtypeStruct((n, PA_COLS), F32),
        scratch_shapes=[
            pltpu.VMEM((IN_COLS - A_OFF, D_MODEL), F32),
            pltpu.VMEM((PA_COLS, D_MODEL), BF16),
            pltpu.SemaphoreType.DMA((2,)),
        ],
        compiler_params=_params(("arbitrary",)),
        name="in_proj_pa",
    )(xb, w_in_t)


def _mixer_kernel(*refs, n_seq, n_blk, n_chunks, valid, n_hist, has_init, fill_tail):
    if has_init:
        (qkvg_ref, pa_ref, wfu_ref, bf_ref, gain_ref, wpool_ref, pscale_ref, s0_ref, h0_ref, _mixed_in,
         mixed_ref, sout_ref, hout_ref, s_scr, z_scr) = refs
    else:
        (qkvg_ref, pa_ref, wfu_ref, bf_ref, gain_ref, wpool_ref, pscale_ref,
         mixed_ref, sout_ref, hout_ref, s_scr, z_scr) = refs
    C = CHUNK
    step = pl.program_id(0)
    blk = step % n_blk

    def pad_rows(x):
        if valid == C:
            return x
        return jnp.concatenate([x, jnp.zeros((C - valid, x.shape[1]), x.dtype)], axis=0)

    def process():
        @pl.when(blk == 0)
        def _():
            if has_init:
                s_scr[...] = s0_ref[...]
                z_scr[0:C - HIST_ROWS, :] = jnp.zeros((C - HIST_ROWS, POOL_WIDTH), F32)
                z_scr[C - HIST_ROWS:C, :] = h0_ref[...]
            else:
                s_scr[...] = jnp.zeros(s_scr.shape, F32)
                z_scr[...] = jnp.zeros(z_scr.shape, F32)

        row = lax.broadcasted_iota(jnp.int32, (C, C), 0)
        col = lax.broadcasted_iota(jnp.int32, (C, C), 1)
        causal = row >= col
        tri = causal.astype(BF16)
        t_idx = lax.broadcasted_iota(jnp.int32, (C, 2 * C), 0)
        s_idx = lax.broadcasted_iota(jnp.int32, (C, 2 * C), 1) - C
        lag = t_idx - s_idx
        bands = [((lag >= 0) & (lag < w)).astype(BF16) for w in POOL_WINDOWS]

        def chunk(j, carry):
            if n_chunks == 1:
                rows = slice(0, valid)
            else:
                rows = pl.ds(pl.multiple_of(j * C, C), C)
            qkvg = pad_rows(qkvg_ref[rows, :])
            pa = pad_rows(pa_ref[rows, :])
            q = qkvg[:, Q_OFF:K_OFF].astype(F32)
            k = qkvg[:, K_OFF:V_OFF].astype(F32)
            v = qkvg[:, V_OFF:G_OFF]
            g = qkvg[:, G_OFF:A_OFF].astype(F32)
            zc = pa[:, :POOL_WIDTH]
            a = pa[:, POOL_WIDTH:]

            la = jax.nn.log_sigmoid(_dot(a.astype(BF16), wfu_ref[...]) + bf_ref[...]) * (1.0 / GATE_TAU)
            if valid != C:
                la = jnp.where(lax.broadcasted_iota(jnp.int32, la.shape, 0) < valid, la, 0.0)
            bcum = _dot_split(tri, la)
            b_last = bcum[C - 1:C, :]
            q_t = (q * jnp.exp(bcum) * (GLA_DK ** -0.5)).astype(BF16)
            k_t = (k * jnp.exp(-bcum)).astype(BF16)
            k_dec = k * jnp.exp(b_last - bcum)
            e_last = jnp.exp(b_last)
            outs = []
            for h in range(GLA_HEADS):
                ks = slice(h * GLA_DK, (h + 1) * GLA_DK)
                vs = slice(h * GLA_DV, (h + 1) * GLA_DV)
                s_h = s_scr[h]
                att = jnp.where(causal, _dot_nt(q_t[:, ks], k_t[:, ks]), 0.0)
                o = _dot(att.astype(BF16), v[:, vs]) + _dot(q_t[:, ks], s_h.astype(BF16))
                k_aug = jnp.concatenate([k_dec[:, ks], jnp.zeros((GLA_DK - C, GLA_DK), F32)], axis=0)
                v_aug = jnp.concatenate([v[:, vs], jnp.zeros((GLA_DK - C, GLA_DV), BF16)], axis=0)
                decay = jnp.broadcast_to(e_last[:, ks], (GLA_DK, GLA_DK)).T
                decay = jnp.concatenate([decay] * (GLA_DV // GLA_DK), axis=1)
                s_scr[h] = decay * s_h + _dot(k_aug.T.astype(BF16), v_aug)
                outs.append(o * lax.rsqrt(jnp.mean(o * o, axis=-1, keepdims=True) + RMS_EPS))
            gla = jnp.concatenate(outs, axis=1) * gain_ref[...] * (g * jax.nn.sigmoid(g))

            z_ext = jnp.concatenate([z_scr[...], zc], axis=0)
            t0 = (blk * n_chunks + j) * C + n_hist + 1
            avail = lax.broadcasted_iota(jnp.int32, (C, POOL_GC), 0) + t0
            pooled = []
            for gi, w in enumerate(POOL_WINDOWS):
                cs = slice(gi * POOL_GC, (gi + 1) * POOL_GC)
                cnt = jnp.minimum(avail, w).astype(F32)
                m = _dot_split(bands[gi], z_ext[:, cs]) / cnt - zc[:, cs]
                pooled.append(_dot(m.astype(BF16), wpool_ref[gi]))
            pool = jnp.concatenate(pooled, axis=1) * pscale_ref[...]
            z_scr[...] = zc

            res = jnp.concatenate([gla, pool], axis=1).astype(BF16)
            mixed_ref[rows, :] = res[0:valid]
            return carry

        if n_chunks == 1:
            chunk(0, 0)
        else:
            lax.fori_loop(0, n_chunks, chunk, 0, unroll=2)

        @pl.when(blk == n_blk - 1)
        def _():
            sout_ref[...] = s_scr[...]
            hout_ref[...] = z_scr[valid - HIST_ROWS:valid, :]

    if fill_tail:
        pl.when(step < n_seq * n_blk)(process)

        @pl.when(step == n_seq * n_blk)
        def _():
            mixed_ref[...] = jnp.zeros(mixed_ref.shape, BF16)
    else:
        process()


def _mixer(qkvg, pa, wfu, bfg, gain, wpool, pscale, *, row0, n_seq, seq_len, n_hist, init=None, mixed_in=None):
    n = qkvg.shape[0]
    if seq_len % CHUNK == 0:
        rb, valid = _row_tile(seq_len, (512, 256, 128, 64)), CHUNK
    else:
        rb, valid = seq_len, seq_len
    assert seq_len % rb == 0 and row0 % rb == 0 and valid >= HIST_ROWS
    n_blk = seq_len // rb
    n_chunks = rb // valid
    base = row0 // rb
    n_steps = n_seq * n_blk
    has_init = init is not None
    assert has_init == (mixed_in is not None)
    tail_rows = n - (row0 + n_seq * seq_len)
    fill_tail = (not has_init) and tail_rows > 0
    assert tail_rows <= rb

    rows = lambda t: (base + t, 0)
    const2 = lambda t: (0, 0)
    seq3 = lambda t: (jnp.minimum(t // n_blk, n_seq - 1), 0, 0)
    seq4 = lambda t: (jnp.minimum(t // n_blk, n_seq - 1), 0, 0, 0)

    in_specs = [
        pl.BlockSpec((rb, A_OFF), rows),
        pl.BlockSpec((rb, PA_COLS), rows),
        pl.BlockSpec((LANE, GLA_KW), const2),
        pl.BlockSpec((1, GLA_KW), const2),
        pl.BlockSpec((1, GLA_WIDTH), const2),
        pl.BlockSpec((POOL_GROUPS, POOL_GC, POOL_GC), lambda t: (0, 0, 0)),
        pl.BlockSpec((1, POOL_WIDTH), const2),
    ]
    args = [qkvg, pa, wfu, bfg, gain, wpool, pscale]
    aliases = {}
    if has_init:
        in_specs += [
            pl.BlockSpec((None, GLA_HEADS, GLA_DK, GLA_DV), seq4),
            pl.BlockSpec((None, HIST_ROWS, POOL_WIDTH), seq3),
            pl.BlockSpec(memory_space=pl.ANY),
        ]
        args += [init[0], init[1], mixed_in]
        aliases = {len(args) - 1: 0}
    return pl.pallas_call(
        functools.partial(_mixer_kernel, n_seq=n_seq, n_blk=n_blk, n_chunks=n_chunks, valid=valid, n_hist=n_hist,
                          has_init=has_init, fill_tail=fill_tail),
        grid=(n_steps + int(fill_tail),),
        in_specs=in_specs,
        out_specs=[
            pl.BlockSpec((rb, D_MODEL), rows),
            pl.BlockSpec((None, GLA_HEADS, GLA_DK, GLA_DV), seq4),
            pl.BlockSpec((None, HIST_ROWS, POOL_WIDTH), seq3),
        ],
        out_shape=[
            jax.ShapeDtypeStruct((n, D_MODEL), BF16),
            jax.ShapeDtypeStruct((n_seq, GLA_HEADS, GLA_DK, GLA_DV), F32),
            jax.ShapeDtypeStruct((n_seq, HIST_ROWS, POOL_WIDTH), F32),
        ],
        scratch_shapes=[
            pltpu.VMEM((GLA_HEADS, GLA_DK, GLA_DV), F32),
            pltpu.VMEM((CHUNK, POOL_WIDTH), F32),
        ],
        input_output_aliases=aliases,
        compiler_params=_params(("arbitrary",)),
        name="mixer_init" if has_init else "mixer",
    )(*args)


def _route(logits):
    lane = lax.broadcasted_iota(jnp.int32, logits.shape, 1).astype(F32)
    neg = -jnp.inf
    first_lane = lambda hit: jnp.min(jnp.where(hit, lane, float(ROUTE_LANES)), axis=-1, keepdims=True)
    lg = jnp.where(lane < N_GROUPS, logits, neg)
    un = jnp.exp(lg - jnp.max(lg, axis=-1, keepdims=True))
    pg = un / jnp.sum(un, axis=-1, keepdims=True)
    top_pg = jnp.max(pg, axis=-1, keepdims=True)
    gsel = first_lane((pg == top_pg) & (lane < N_GROUPS))
    first = N_GROUPS + gsel * N_EXP
    le = jnp.where((lane >= first) & (lane < first + N_EXP), logits, neg)
    v0 = jnp.max(le, axis=-1, keepdims=True)
    i0 = first_lane(le == v0)
    le = jnp.where(lane == i0, neg, le)
    v1 = jnp.max(le, axis=-1, keepdims=True)
    i1 = first_lane(le == v1)
    u1 = jnp.exp(v1 - v0)
    den = 1.0 + u1
    e0 = (i0 - N_GROUPS).astype(jnp.int32)
    e1 = (i1 - N_GROUPS).astype(jnp.int32)
    return e0, e1, (1.0 / den) * top_pg, (u1 / den) * top_pg


def _out_proj_kernel(mixed_ref, x_ref, w_ref, g_ref, b_ref, wr_ref, br_ref, x1_ref, x1p_ref, ri_ref, rw_ref,
                     wrs_ref):
    tm = x_ref.shape[0]

    @pl.when(pl.program_id(0) == 0)
    def _():
        wr = wr_ref[...]
        wr_hi = wr.astype(BF16)
        wrs_ref[:, :ROUTE_LANES] = wr_hi
        wrs_ref[:, ROUTE_LANES:] = (wr - wr_hi.astype(F32)).astype(BF16)

    y = _dot(mixed_ref[...], w_ref[...])
    x1 = _layer_norm(DN_ALPHA * x_ref[...] + y, g_ref[...], b_ref[...])
    x1_ref[...] = x1
    x1p_ref[...] = _pack_pairs(x1)
    x_hi = x1.astype(BF16)
    x_lo = (x1 - x_hi.astype(F32)).astype(BF16)
    prod = _dot(jnp.concatenate([x_hi, x_lo], axis=0), wrs_ref[...])
    corr = prod[tm:, :ROUTE_LANES] + prod[:tm, ROUTE_LANES:]
    logits = prod[:tm, :ROUTE_LANES] + ROUTER_CORRECTION * corr + br_ref[...]
    e0, e1, w0, w1 = _route(logits)
    lane = lax.broadcasted_iota(jnp.int32, logits.shape, 1)
    ri_ref[...] = jnp.where(lane == 0, e0, jnp.where(lane == 1, e1, 0))
    rw_ref[...] = jnp.where(lane == 0, w0, jnp.where(lane == 1, w1, 0.0))


def _out_proj(mixed, x, w_out, g, b, wr, br, layer):
    n = x.shape[0]
    tm = _row_tile(n, (640, 416, 320, 256, 128))
    row = lambda i: (i, 0)
    const = lambda i: (0, 0)
    return pl.pallas_call(
        _out_proj_kernel,
        grid=(n // tm,),
        in_specs=[
            pl.BlockSpec((tm, D_MODEL), row),
            pl.BlockSpec((tm, D_MODEL), row),
            pl.BlockSpec((None, D_MODEL, D_MODEL), lambda i: (layer, 0, 0), pipeline_mode=pl.Buffered(1)),
            pl.BlockSpec((1, D_MODEL), const),
            pl.BlockSpec((1, D_MODEL), const),
            pl.BlockSpec((D_MODEL, ROUTE_LANES), const),
            pl.BlockSpec((1, ROUTE_LANES), const),
        ],
        out_specs=[
            pl.BlockSpec((tm, D_MODEL), row),
            pl.BlockSpec((tm, D_MODEL // 2), row),
            pl.BlockSpec((tm, ROUTE_LANES), row),
            pl.BlockSpec((tm, ROUTE_LANES), row),
        ],
        out_shape=[
            jax.ShapeDtypeStruct((n, D_MODEL), F32),
            jax.ShapeDtypeStruct((n, D_MODEL // 2), jnp.uint32),
            jax.ShapeDtypeStruct((n, ROUTE_LANES), jnp.int32),
            jax.ShapeDtypeStruct((n, ROUTE_LANES), F32),
        ],
        scratch_shapes=[pltpu.VMEM((D_MODEL, 2 * ROUTE_LANES), BF16)],
        compiler_params=_params(("arbitrary",)),
        name="out_proj_ln_route",
    )(mixed, x, w_out, g, b, wr, br)


def _dispatch_plan(eid):
    n_pairs = eid.shape[0] * 2
    n_tiles = n_pairs // EXPERT_TILE
    flat = eid.reshape(-1)
    onehot = (flat[:, None] == jnp.arange(N_EXPERTS, dtype=jnp.int32)[None, :]).astype(jnp.int32)
    csum = jnp.cumsum(onehot, axis=0)
    rank = jnp.sum(csum * onehot, axis=1) - 1
    counts = csum[-1]
    offs = jnp.concatenate([jnp.zeros((1,), jnp.int32), jnp.cumsum(counts)])
    pos = (offs[flat] + rank).astype(jnp.int32)
    bounds = jnp.sort(jnp.concatenate([jnp.arange(n_tiles, dtype=jnp.int32) * EXPERT_TILE, offs[:N_EXPERTS]]))
    seg_lo = bounds
    seg_hi = jnp.concatenate([bounds[1:], jnp.full((1,), n_pairs, jnp.int32)])
    tile = jnp.minimum(seg_lo // EXPERT_TILE, n_tiles - 1)
    expert = jnp.minimum(jnp.searchsorted(offs[1:], seg_lo, side="right"), N_EXPERTS - 1).astype(jnp.int32)
    lo = seg_lo - tile * EXPERT_TILE
    hi = jnp.where(seg_hi > seg_lo, seg_hi - tile * EXPERT_TILE, lo)
    valid = seg_hi > seg_lo
    e_seen = lax.cummax(jnp.where(valid, expert, -1))
    prev = jnp.concatenate([jnp.full((1,), -1, jnp.int32), e_seen[:-1]])
    fetch = valid & (expert > prev)
    run = jnp.maximum(jnp.cumsum(fetch.astype(jnp.int32)) - 1, 0)
    slot = run % 2
    fetch = fetch.astype(jnp.int32) * jnp.where(run == 0, 2, 1)
    ids = jnp.arange(N_EXPERTS, dtype=jnp.int32)
    later = lax.cummin(jnp.where(counts > 0, ids, N_EXPERTS), reverse=True)
    nxt_of = jnp.concatenate([later[1:], jnp.full((1,), N_EXPERTS, jnp.int32)])
    nxt = jnp.where(nxt_of[expert] < N_EXPERTS, nxt_of[expert], -1)
    sched = (fetch, slot.astype(jnp.int32), nxt.astype(jnp.int32))
    return pos, (tile.astype(jnp.int32), expert, lo.astype(jnp.int32), hi.astype(jnp.int32)) + sched


def _dispatch_kernel(pos_ref, x_ref, xs_hbm, sem, *, tb):
    base = pl.program_id(0) * tb

    def scatter(t, k):
        return pltpu.make_async_copy(x_ref.at[pl.ds(t, 1)], xs_hbm.at[pl.ds(pos_ref[2 * (base + t) + k], 1)], sem)

    def issue(t, c):
        scatter(t, 0).start()
        scatter(t, 1).start()
        return c

    lax.fori_loop(0, tb, issue, 0, unroll=8)

    def drain(t, c):
        scatter(t, 0).wait()
        scatter(t, 1).wait()
        return c

    lax.fori_loop(0, tb, drain, 0, unroll=8)


def _dispatch(pos, x1):
    n, width = x1.shape
    tb = _row_tile(n)
    return pl.pallas_call(
        functools.partial(_dispatch_kernel, tb=tb),
        grid_spec=pltpu.PrefetchScalarGridSpec(
            num_scalar_prefetch=1,
            grid=(n // tb,),
            in_specs=[pl.BlockSpec((tb, width), lambda i, p: (i, 0))],
            out_specs=pl.BlockSpec(memory_space=pl.ANY),
            scratch_shapes=[pltpu.SemaphoreType.DMA(())],
        ),
        out_shape=jax.ShapeDtypeStruct((2 * n, width), x1.dtype),
        compiler_params=_params(("arbitrary",)),
        name="moe_dispatch",
    )(pos, x1)


def _experts_kernel(tile_ref, exp_ref, lo_ref, hi_ref, fetch_ref, slot_ref, nxt_ref,
                    xs_ref, w1_hbm, w3_hbm, w2_hbm, o_ref, w1_buf, w3_buf, w2_buf, sem, *, layer):
    i = pl.program_id(0)
    lo, hi = lo_ref[i], hi_ref[i]
    slot = slot_ref[i]

    def weight_copies(expert, s):
        e = layer * N_EXPERTS + expert
        return (pltpu.make_async_copy(w1_hbm.at[e], w1_buf.at[s], sem.at[s, 0]),
                pltpu.make_async_copy(w3_hbm.at[e], w3_buf.at[s], sem.at[s, 1]),
                pltpu.make_async_copy(w2_hbm.at[e], w2_buf.at[s], sem.at[s, 2]))

    @pl.when(fetch_ref[i] == 2)
    def _():
        for c in weight_copies(exp_ref[i], slot):
            c.start()

    @pl.when(fetch_ref[i] > 0)
    def _():
        for c in weight_copies(exp_ref[i], slot):
            c.wait()

        @pl.when(nxt_ref[i] >= 0)
        def _():
            for c in weight_copies(nxt_ref[i], 1 - slot):
                c.start()

    def compute():
        x = _unpack_pairs(xs_ref[...]).astype(BF16)
        h1 = _dot(x, w1_buf[slot].astype(BF16))
        h3 = _dot(x, w3_buf[slot].astype(BF16))
        a = (h1 * jax.nn.sigmoid(h1) * h3).astype(BF16)
        return _pack_pairs(_dot(a, w2_buf[slot].astype(BF16)))

    @pl.when((hi > lo) & (lo == 0))
    def _():
        o_ref[...] = compute()

    @pl.when((hi > lo) & (lo > 0))
    def _():
        r = lax.broadcasted_iota(jnp.int32, o_ref.shape, 0)
        o_ref[...] = jnp.where((r >= lo) & (r < hi), compute(), o_ref[...])


def _experts(plan, xs, w1, w3, w2, layer):
    n_items = plan[0].shape[0]
    xrow = lambda i, t, *_: (t[i], 0)
    any_spec = pl.BlockSpec(memory_space=pl.ANY)
    return pl.pallas_call(
        functools.partial(_experts_kernel, layer=layer),
        grid_spec=pltpu.PrefetchScalarGridSpec(
            num_scalar_prefetch=len(plan),
            grid=(n_items,),
            in_specs=[pl.BlockSpec((EXPERT_TILE, D_MODEL // 2), xrow), any_spec, any_spec, any_spec],
            out_specs=pl.BlockSpec((EXPERT_TILE, D_MODEL // 2), xrow),
            scratch_shapes=[
                pltpu.VMEM((2, D_MODEL, EXPERT_HIDDEN), F32),
                pltpu.VMEM((2, D_MODEL, EXPERT_HIDDEN), F32),
                pltpu.VMEM((2, EXPERT_HIDDEN, D_MODEL), F32),
                pltpu.SemaphoreType.DMA((2, 3)),
            ],
        ),
        out_shape=jax.ShapeDtypeStruct(xs.shape, xs.dtype),
        compiler_params=_params(("arbitrary",)),
        name="moe_experts",
    )(*plan, xs, w1, w3, w2)


def _combine_kernel(pos_ref, x1_ref, rw_ref, ys_hbm, g_ref, b_ref, *rest, tm, n_prompt_blocks):
    if n_prompt_blocks is None:
        x2_ref, x2b_ref, buf, sem = rest
    else:
        yp_ref, ysm_ref, buf, sem = rest
    i = pl.program_id(0)
    base = i * tm

    def gather(t, k):
        src = ys_hbm.at[pl.ds(pos_ref[2 * (base + t) + k], 1)]
        return pltpu.make_async_copy(src, buf.at[k, pl.ds(t, 1)], sem)

    def issue(t, c):
        gather(t, 0).start()
        gather(t, 1).start()
        return c

    lax.fori_loop(0, tm, issue, 0, unroll=8)

    def drain(t, c):
        gather(t, 0).wait()
        gather(t, 1).wait()
        return c

    lax.fori_loop(0, tm, drain, 0, unroll=8)

    rw = rw_ref[...]
    f = rw[:, 0:1] * _unpack_pairs(buf[0]) + rw[:, 1:2] * _unpack_pairs(buf[1])
    x2 = _layer_norm(DN_ALPHA * x1_ref[...] + f, g_ref[...], b_ref[...])
    if n_prompt_blocks is None:
        x2_ref[...] = x2
        x2b_ref[...] = x2.astype(BF16)
    else:
        @pl.when(i < n_prompt_blocks)
        def _():
            yp_ref[...] = x2

        @pl.when(i >= n_prompt_blocks)
        def _():
            ysm_ref[...] = x2


def _combine(pos, x1, rw, ys, g, b, *, split=None):
    n = x1.shape[0]
    row = lambda i, p: (i, 0)
    const = lambda i, p: (0, 0)
    if split is None:
        tm, npb = _row_tile(n, (320, 256, 128)), None
        out_specs = [pl.BlockSpec((tm, D_MODEL), row), pl.BlockSpec((tm, D_MODEL), row)]
        out_shape = [jax.ShapeDtypeStruct((n, D_MODEL), F32), jax.ShapeDtypeStruct((n, D_MODEL), BF16)]
    else:
        n_p, n_s = split
        tm = n_s
        assert n_p % tm == 0 and n_p + n_s == n
        npb = n_p // tm
        out_specs = [
            pl.BlockSpec((tm, D_MODEL), lambda i, p: (jnp.minimum(i, npb - 1), 0)),
            pl.BlockSpec((tm, D_MODEL), lambda i, p: (jnp.maximum(i - npb, 0), 0)),
        ]
        out_shape = [jax.ShapeDtypeStruct((n_p, D_MODEL), F32), jax.ShapeDtypeStruct((n_s, D_MODEL), F32)]
    return pl.pallas_call(
        functools.partial(_combine_kernel, tm=tm, n_prompt_blocks=npb),
        grid_spec=pltpu.PrefetchScalarGridSpec(
            num_scalar_prefetch=1,
            grid=(n // tm,),
            in_specs=[
                pl.BlockSpec((tm, D_MODEL), row),
                pl.BlockSpec((tm, ROUTE_LANES), row),
                pl.BlockSpec(memory_space=pl.ANY),
                pl.BlockSpec((1, D_MODEL), const),
                pl.BlockSpec((1, D_MODEL), const),
            ],
            out_specs=out_specs,
            scratch_shapes=[pltpu.VMEM((2, tm, D_MODEL // 2), jnp.uint32), pltpu.SemaphoreType.DMA(())],
        ),
        out_shape=out_shape,
        compiler_params=_params(("arbitrary",)),
        name="moe_combine_ln" if split is None else "moe_combine_ln_final",
    )(pos, x1, rw, ys, g, b)


def kernel(x_prompt, x_sample, state_gla, cache_pool, ln_in_g, ln_in_b, w_in, w_forget_up, b_forget, gla_norm_g, w_pool, pool_scale, w_out, ln1_g, ln1_b, router_group_w, router_group_b, router_expert_w, router_expert_b, w_exp_gate, w_exp_up, w_exp_down, ln2_g, ln2_b):
    n_pb, seq, d = x_prompt.shape
    n_sb, dseq, _ = x_sample.shape
    n_p, n_s = n_pb * seq, n_sb * dseq
    depth = w_in.shape[0]
    row2 = lambda v: v.reshape(1, -1)

    w_out_b = w_out.astype(BF16)
    w_in_t = jnp.swapaxes(w_in, 1, 2)
    wfu = jnp.concatenate([w_forget_up, jnp.zeros((depth, LANE - GATE_RANK, GLA_KW), F32)], axis=1).astype(BF16)
    wpool = w_pool.astype(BF16)
    wr = jnp.concatenate(
        [router_group_w, router_expert_w.transpose(0, 2, 1, 3).reshape(depth, d, N_EXPERTS),
         jnp.zeros((depth, d, ROUTE_LANES - N_GROUPS - N_EXPERTS), F32)], axis=2)
    br = jnp.concatenate(
        [router_group_b, router_expert_b.reshape(depth, N_EXPERTS),
         jnp.zeros((depth, ROUTE_LANES - N_GROUPS - N_EXPERTS), F32)], axis=1)
    w1 = w_exp_gate.reshape(depth * N_EXPERTS, d, EXPERT_HIDDEN)
    w3 = w_exp_up.reshape(depth * N_EXPERTS, d, EXPERT_HIDDEN)
    w2 = w_exp_down.reshape(depth * N_EXPERTS, EXPERT_HIDDEN, d)
    hist0 = jnp.concatenate([jnp.zeros((depth, n_sb, 1, POOL_WIDTH), F32), cache_pool], axis=2)

    x, xb = _ln_in(x_prompt.reshape(n_p, d), x_sample.reshape(n_s, d), row2(ln_in_g), row2(ln_in_b))
    states_p, hists_p, states_s, hists_s = [], [], [], []
    for l in range(depth):
        qkvg = _in_proj_qkvg(xb, w_in_t, l)
        pa = _in_proj_pa(xb, w_in_t, l)
        mix_w = (wfu[l], row2(b_forget[l]), row2(gla_norm_g[l]), wpool[l], row2(pool_scale[l]))
        mixed, sp, hp = _mixer(qkvg, pa, *mix_w, row0=0, n_seq=n_pb, seq_len=seq, n_hist=0)
        mixed, ss, hs = _mixer(qkvg, pa, *mix_w, row0=n_p, n_seq=n_sb, seq_len=dseq, n_hist=POOL_HIST,
                               init=(state_gla[l], hist0[l]), mixed_in=mixed)
        x1, x1p, ri, rw = _out_proj(mixed, x, w_out_b, row2(ln1_g[l]), row2(ln1_b[l]), wr[l], row2(br[l]), l)
        pos, plan = _dispatch_plan(ri[:, :2])
        xs = _dispatch(pos, x1p)
        ys = _experts(plan, xs, w1, w3, w2, l)
        if l + 1 < depth:
            x, xb = _combine(pos, x1, rw, ys, row2(ln2_g[l]), row2(ln2_b[l]))
        else:
            y_p, y_s = _combine(pos, x1, rw, ys, row2(ln2_g[l]), row2(ln2_b[l]), split=(n_p, n_s))
        states_p.append(sp)
        hists_p.append(hp[:, 1:])
        states_s.append(ss)
        hists_s.append(hs[:, 1:])
    return (y_p.reshape(n_pb, seq, d), y_s.reshape(n_sb, dseq, d),
            jnp.stack(states_p), jnp.stack(hists_p), jnp.stack(states_s), jnp.stack(hists_s))
```

```python
import functools

import jax
import jax.numpy as jnp
from jax import lax
from jax.experimental import pallas as pl
from jax.experimental.pallas import tpu as pltpu

D_MODEL = 2048
DEPTH = 4
CHUNK = 64

GLA_HEADS = 4
GLA_WIDTH = D_MODEL // 2
GLA_DV = GLA_WIDTH // GLA_HEADS
GLA_DK = GLA_DV // 2
GLA_KW = GLA_HEADS * GLA_DK
GATE_RANK = 16
GATE_TAU = 16.0

POOL_WINDOWS = (2, 4, 8, 16)
POOL_GROUPS = len(POOL_WINDOWS)
POOL_WIDTH = D_MODEL - GLA_WIDTH
POOL_GC = POOL_WIDTH // POOL_GROUPS
POOL_HIST = max(POOL_WINDOWS) - 1
HIST_ROWS = POOL_HIST + 1

Q_OFF = 0
K_OFF = Q_OFF + GLA_KW
V_OFF = K_OFF + GLA_KW
G_OFF = V_OFF + GLA_WIDTH
A_OFF = G_OFF + GLA_WIDTH
P_OFF = A_OFF + GATE_RANK
IN_COLS = P_OFF + POOL_WIDTH
PA_COLS = POOL_WIDTH + 128

N_GROUPS = 4
N_EXP = 8
N_EXPERTS = N_GROUPS * N_EXP
EXPERT_HIDDEN = D_MODEL // 4
ROUTE_LANES = 128
ROUTER_CORRECTION = 0.5

DN_ALPHA = (2 * DEPTH) ** 0.25
LN_EPS = 1e-5
RMS_EPS = 1e-6

VMEM_LIMIT_BYTES = 56 * 1024 * 1024
LANE = 128
EXPERT_TILE = 256
BF16 = jnp.bfloat16
F32 = jnp.float32


def _params(sem, vmem=VMEM_LIMIT_BYTES):
    return pltpu.CompilerParams(dimension_semantics=sem, vmem_limit_bytes=vmem)


def _row_tile(n, prefs=(640, 512, 320, 256, 128)):
    for t in prefs:
        if n % t == 0:
            return t
    raise ValueError(f"no row tile for {n} rows")


def _layer_norm(x, g, b):
    mu = jnp.mean(x, axis=-1, keepdims=True)
    xc = x - mu
    var = jnp.mean(xc * xc, axis=-1, keepdims=True)
    return xc * lax.rsqrt(var + LN_EPS) * g + b


def _dot(a, b):
    return jnp.dot(a, b, preferred_element_type=F32)


def _dot_nt(a, b):
    return lax.dot_general(a, b, (((1,), (1,)), ((), ())), preferred_element_type=F32)


def _pack_pairs(x):
    k = x.shape[1] // 2
    hi = lax.bitcast_convert_type(x[:, :k].astype(BF16).astype(F32), jnp.uint32)
    lo = lax.bitcast_convert_type(x[:, k:].astype(BF16).astype(F32), jnp.uint32)
    return hi | (lo >> 16)


def _unpack_pairs(w):
    hi = lax.bitcast_convert_type(w & jnp.uint32(0xFFFF0000), F32)
    lo = lax.bitcast_convert_type(w << 16, F32)
    return jnp.concatenate([hi, lo], axis=1)


def _dot_split(m_bf16, x):
    hi = x.astype(BF16)
    lo = (x - hi.astype(F32)).astype(BF16)
    return _dot(m_bf16, hi) + _dot(m_bf16, lo)


def _ln_in_kernel(xp_ref, xs_ref, g_ref, b_ref, x_ref, xb_ref, *, n_prompt_blocks):
    i = pl.program_id(0)
    x = jnp.where(i < n_prompt_blocks, xp_ref[...], xs_ref[...])
    y = _layer_norm(x, g_ref[...], b_ref[...])
    x_ref[...] = y
    xb_ref[...] = y.astype(BF16)


def _ln_in(xp, xs, g, b):
    n_p, n_s = xp.shape[0], xs.shape[0]
    tm = n_s
    assert n_p % tm == 0
    npb = n_p // tm
    n = n_p + n_s
    return pl.pallas_call(
        functools.partial(_ln_in_kernel, n_prompt_blocks=npb),
        grid=(npb + 1,),
        in_specs=[
            pl.BlockSpec((tm, D_MODEL), lambda i: (jnp.minimum(i, npb - 1), 0)),
            pl.BlockSpec((tm, D_MODEL), lambda i: (0, 0)),
            pl.BlockSpec((1, D_MODEL), lambda i: (0, 0)),
            pl.BlockSpec((1, D_MODEL), lambda i: (0, 0)),
        ],
        out_specs=[
            pl.BlockSpec((tm, D_MODEL), lambda i: (i, 0)),
            pl.BlockSpec((tm, D_MODEL), lambda i: (i, 0)),
        ],
        out_shape=[jax.ShapeDtypeStruct((n, D_MODEL), F32), jax.ShapeDtypeStruct((n, D_MODEL), BF16)],
        compiler_params=_params(("arbitrary",)),
        name="ln_in",
    )(xp, xs, g, b)


def _proj_qkvg_kernel(x_ref, wt_ref, o_ref, wb_ref):
    @pl.when(pl.program_id(1) == 0)
    def _():
        wb_ref[...] = wt_ref[...].astype(BF16)

    o_ref[...] = _dot_nt(x_ref[...], wb_ref[...]).astype(o_ref.dtype)


def _in_proj_qkvg(xb, w_in_t, layer):
    n = xb.shape[0]
    tm, tn = _row_tile(n), 1024
    return pl.pallas_call(
        _proj_qkvg_kernel,
        grid=(A_OFF // tn, n // tm),
        in_specs=[
            pl.BlockSpec((tm, D_MODEL), lambda j, m: (m, 0)),
            pl.BlockSpec((None, tn, D_MODEL), lambda j, m: (layer, j, 0)),
        ],
        out_specs=pl.BlockSpec((tm, tn), lambda j, m: (m, j)),
        out_shape=jax.ShapeDtypeStruct((n, A_OFF), BF16),
        scratch_shapes=[pltpu.VMEM((tn, D_MODEL), BF16)],
        compiler_params=_params(("arbitrary", "arbitrary")),
        name="in_proj_qkvg",
    )(xb, w_in_t)


def _proj_pa_kernel(x_ref, wt_hbm, o_ref, stage_ref, wb_ref, sem, *, layer):
    n_tail = IN_COLS - A_OFF

    @pl.when(pl.program_id(0) == 0)
    def _():
        copies = (
            pltpu.make_async_copy(wt_hbm.at[layer, pl.ds(P_OFF, POOL_WIDTH)], stage_ref.at[pl.ds(0, POOL_WIDTH)],
                                  sem.at[0]),
            pltpu.make_async_copy(wt_hbm.at[layer, pl.ds(A_OFF, GATE_RANK)],
                                  stage_ref.at[pl.ds(POOL_WIDTH, GATE_RANK)], sem.at[1]),
        )
        for c in copies:
            c.start()
        for c in copies:
            c.wait()
        wb_ref[0:n_tail, :] = stage_ref[...].astype(BF16)
        wb_ref[n_tail:, :] = jnp.zeros((PA_COLS - n_tail, D_MODEL), BF16)

    o_ref[...] = _dot_nt(x_ref[...], wb_ref[...])


def _in_proj_pa(xb, w_in_t, layer):
    n = xb.shape[0]
    tm = _row_tile(n)
    return pl.pallas_call(
        functools.partial(_proj_pa_kernel, layer=layer),
        grid=(n // tm,),
        in_specs=[
            pl.BlockSpec((tm, D_MODEL), lambda m: (m, 0)),
            pl.BlockSpec(memory_space=pl.ANY),
        ],
        out_specs=pl.BlockSpec((tm, PA_COLS), lambda m: (m, 0)),
        out_shape=jax.ShapeDtypeStruct((n, PA_COLS), F32),
        scratch_shapes=[
            pltpu.VMEM((IN_COLS - A_OFF, D_MODEL), F32),
            pltpu.VMEM((PA_COLS, D_MODEL), BF16),
            pltpu.SemaphoreType.DMA((2,)),
        ],
        compiler_params=_params(("arbitrary",)),
        name="in_proj_pa",
    )(xb, w_in_t)


def _mixer_kernel(*refs, n_seq, n_blk, n_chunks, valid, n_hist, has_init, fill_tail):
    if has_init:
        (qkvg_ref, pa_ref, wfu_ref, bf_ref, gain_ref, wpool_ref, pscale_ref, s0_ref, h0_ref, _mixed_in,
         mixed_ref, sout_ref, hout_ref, s_scr, z_scr) = refs
    else:
        (qkvg_ref, pa_ref, wfu_ref, bf_ref, gain_ref, wpool_ref, pscale_ref,
         mixed_ref, sout_ref, hout_ref, s_scr, z_scr) = refs
    C = CHUNK
    step = pl.program_id(0)
    blk = step % n_blk

    def pad_rows(x):
        if valid == C:
            return x
        return jnp.concatenate([x, jnp.zeros((C - valid, x.shape[1]), x.dtype)], axis=0)

    def process():
        @pl.when(blk == 0)
        def _():
            if has_init:
                s_scr[...] = s0_ref[...]
                z_scr[0:C - HIST_ROWS, :] = jnp.zeros((C - HIST_ROWS, POOL_WIDTH), F32)
                z_scr[C - HIST_ROWS:C, :] = h0_ref[...]
            else:
                s_scr[...] = jnp.zeros(s_scr.shape, F32)
                z_scr[...] = jnp.zeros(z_scr.shape, F32)

        row = lax.broadcasted_iota(jnp.int32, (C, C), 0)
        col = lax.broadcasted_iota(jnp.int32, (C, C), 1)
        causal = row >= col
        tri = causal.astype(BF16)
        t_idx = lax.broadcasted_iota(jnp.int32, (C, 2 * C), 0)
        s_idx = lax.broadcasted_iota(jnp.int32, (C, 2 * C), 1) - C
        lag = t_idx - s_idx
        bands = [((lag >= 0) & (lag < w)).astype(BF16) for w in POOL_WINDOWS]

        def chunk(j, carry):
            if n_chunks == 1:
                rows = slice(0, valid)
            else:
                rows = pl.ds(pl.multiple_of(j * C, C), C)
            qkvg = pad_rows(qkvg_ref[rows, :])
            pa = pad_rows(pa_ref[rows, :])
            q = qkvg[:, Q_OFF:K_OFF].astype(F32)
            k = qkvg[:, K_OFF:V_OFF].astype(F32)
            v = qkvg[:, V_OFF:G_OFF]
            g = qkvg[:, G_OFF:A_OFF].astype(F32)
            zc = pa[:, :POOL_WIDTH]
            a = pa[:, POOL_WIDTH:]

            la = jax.nn.log_sigmoid(_dot(a.astype(BF16), wfu_ref[...]) + bf_ref[...]) * (1.0 / GATE_TAU)
            if valid != C:
                la = jnp.where(lax.broadcasted_iota(jnp.int32, la.shape, 0) < valid, la, 0.0)
            bcum = _dot_split(tri, la)
            b_last = bcum[C - 1:C, :]
            q_t = (q * jnp.exp(bcum) * (GLA_DK ** -0.5)).astype(BF16)
            k_t = (k * jnp.exp(-bcum)).astype(BF16)
            k_dec = k * jnp.exp(b_last - bcum)
            e_last = jnp.exp(b_last)
            outs = []
            for h in range(GLA_HEADS):
                ks = slice(h * GLA_DK, (h + 1) * GLA_DK)
                vs = slice(h * GLA_DV, (h + 1) * GLA_DV)
                s_h = s_scr[h]
                att = jnp.where(causal, _dot_nt(q_t[:, ks], k_t[:, ks]), 0.0)
                o = _dot(att.astype(BF16), v[:, vs]) + _dot(q_t[:, ks], s_h.astype(BF16))
                k_aug = jnp.concatenate([k_dec[:, ks], jnp.zeros((GLA_DK - C, GLA_DK), F32)], axis=0)
                v_aug = jnp.concatenate([v[:, vs], jnp.zeros((GLA_DK - C, GLA_DV), BF16)], axis=0)
                decay = jnp.broadcast_to(e_last[:, ks], (GLA_DK, GLA_DK)).T
                decay = jnp.concatenate([decay] * (GLA_DV // GLA_DK), axis=1)
                s_scr[h] = decay * s_h + _dot(k_aug.T.astype(BF16), v_aug)
                outs.append(o * lax.rsqrt(jnp.mean(o * o, axis=-1, keepdims=True) + RMS_EPS))
            gla = jnp.concatenate(outs, axis=1) * gain_ref[...] * (g * jax.nn.sigmoid(g))

            z_ext = jnp.concatenate([z_scr[...], zc], axis=0)
            t0 = (blk * n_chunks + j) * C + n_hist + 1
            avail = lax.broadcasted_iota(jnp.int32, (C, POOL_GC), 0) + t0
            pooled = []
            for gi, w in enumerate(POOL_WINDOWS):
                cs = slice(gi * POOL_GC, (gi + 1) * POOL_GC)
                cnt = jnp.minimum(avail, w).astype(F32)
                m = _dot_split(bands[gi], z_ext[:, cs]) / cnt - zc[:, cs]
                pooled.append(_dot(m.astype(BF16), wpool_ref[gi]))
            pool = jnp.concatenate(pooled, axis=1) * pscale_ref[...]
            z_scr[...] = zc

            res = jnp.concatenate([gla, pool], axis=1).astype(BF16)
            mixed_ref[rows, :] = res[0:valid]
            return carry

        if n_chunks == 1:
            chunk(0, 0)
        else:
            lax.fori_loop(0, n_chunks, chunk, 0, unroll=2)

        @pl.when(blk == n_blk - 1)
        def _():
            sout_ref[...] = s_scr[...]
            hout_ref[...] = z_scr[valid - HIST_ROWS:valid, :]

    if fill_tail:
        pl.when(step < n_seq * n_blk)(process)

        @pl.when(step == n_seq * n_blk)
        def _():
            mixed_ref[...] = jnp.zeros(mixed_ref.shape, BF16)
    else:
        process()


def _mixer_block_kernel(*refs, n_seq, n_blk, rows_valid, n_hist, has_init, fill_tail):
    if has_init:
        (qkvg_ref, pa_ref, wfu_ref, bf_ref, gain_ref, wpool_ref, pscale_ref, s0_ref, h0_ref, _mixed_in,
         mixed_ref, sout_ref, hout_ref, s_scr, z_scr) = refs
    else:
        (qkvg_ref, pa_ref, wfu_ref, bf_ref, gain_ref, wpool_ref, pscale_ref,
         mixed_ref, sout_ref, hout_ref, s_scr, z_scr) = refs
    C = CHUNK
    R = max(rows_valid, 2 * C)
    nc = R // C
    step = pl.program_id(0)
    blk = step % n_blk

    def pad_rows(x):
        if rows_valid == R:
            return x
        return jnp.concatenate([x, jnp.zeros((R - rows_valid, x.shape[1]), x.dtype)], axis=0)

    def process():
        @pl.when(blk == 0)
        def _():
            if has_init:
                s_scr[...] = s0_ref[...]
                z_scr[0:C - HIST_ROWS, :] = jnp.zeros((C - HIST_ROWS, POOL_WIDTH), F32)
                z_scr[C - HIST_ROWS:C, :] = h0_ref[...]
            else:
                s_scr[...] = jnp.zeros(s_scr.shape, F32)
                z_scr[...] = jnp.zeros(z_scr.shape, F32)

        qkvg = pad_rows(qkvg_ref[...])
        pa = pad_rows(pa_ref[...])
        q = qkvg[:, Q_OFF:K_OFF].astype(F32)
        k = qkvg[:, K_OFF:V_OFF].astype(F32)
        v = qkvg[:, V_OFF:G_OFF]
        g = qkvg[:, G_OFF:A_OFF].astype(F32)
        zc = pa[:, :POOL_WIDTH]
        a = pa[:, POOL_WIDTH:]

        row = lax.broadcasted_iota(jnp.int32, (R, R), 0)
        col = lax.broadcasted_iota(jnp.int32, (R, R), 1)
        causal = (row // C == col // C) & (row >= col)
        la = jax.nn.log_sigmoid(_dot(a.astype(BF16), wfu_ref[...]) + bf_ref[...]) * (1.0 / GATE_TAU)
        if rows_valid != R:
            la = jnp.where(lax.broadcasted_iota(jnp.int32, la.shape, 0) < rows_valid, la, 0.0)
        bcum = _dot_split(causal.astype(BF16), la)
        tot = bcum.reshape(nc, C, GLA_KW)[:, C - 1:C, :]
        tot = jnp.broadcast_to(tot, (nc, C, GLA_KW)).reshape(R, GLA_KW)
        q_t = (q * jnp.exp(bcum) * (GLA_DK ** -0.5)).astype(BF16)
        k_t = (k * jnp.exp(-bcum)).astype(BF16)
        k_dec = k * jnp.exp(tot - bcum)
        e_tot = jnp.exp(tot)
        first_half = lax.broadcasted_iota(jnp.int32, (GLA_DK, 2 * C), 1) < C
        outs = []
        for h in range(GLA_HEADS):
            ks = slice(h * GLA_DK, (h + 1) * GLA_DK)
            vs = slice(h * GLA_DV, (h + 1) * GLA_DV)
            att = jnp.where(causal, _dot_nt(q_t[:, ks], k_t[:, ks]), 0.0)
            o_intra = _dot(att.astype(BF16), v[:, vs])
            k_dec_t = k_dec[:, ks].T
            e_tot_t = e_tot[:, ks].T
            updates = []
            for m in range(nc // 2):
                kt_pair = k_dec_t[:, 2 * C * m:2 * C * (m + 1)]
                v_pair = v[2 * C * m:2 * C * (m + 1), vs]
                updates.append(_dot(jnp.where(first_half, kt_pair, 0.0).astype(BF16), v_pair))
                updates.append(_dot(jnp.where(first_half, 0.0, kt_pair).astype(BF16), v_pair))
            s_h = s_scr[h]
            o_inter = []
            for j in range(nc):
                o_inter.append(_dot(q_t[C * j:C * (j + 1), ks], s_h.astype(BF16)))
                decay = jnp.broadcast_to(e_tot_t[:, C * j:C * j + 1], (GLA_DK, GLA_DV))
                s_h = decay * s_h + updates[j]
            s_scr[h] = s_h
            o = o_intra + jnp.concatenate(o_inter, axis=0)
            outs.append(o * lax.rsqrt(jnp.mean(o * o, axis=-1, keepdims=True) + RMS_EPS))
        gla = jnp.concatenate(outs, axis=1) * gain_ref[...] * (g * jax.nn.sigmoid(g))

        z_all = jnp.concatenate([z_scr[...], zc], axis=0)
        z_hi = z_all.astype(BF16)
        z_lo = (z_all - z_hi.astype(F32)).astype(BF16)
        t_idx = lax.broadcasted_iota(jnp.int32, (C, 2 * C), 0)
        s_idx = lax.broadcasted_iota(jnp.int32, (C, 2 * C), 1) - C
        lag = t_idx - s_idx
        avail0 = lax.broadcasted_iota(jnp.int32, (C, POOL_GC), 0) + (blk * rows_valid + n_hist + 1)
        pooled = []
        for gi, w in enumerate(POOL_WINDOWS):
            cs = slice(gi * POOL_GC, (gi + 1) * POOL_GC)
            band = ((lag >= 0) & (lag < w)).astype(BF16)
            means = []
            for j in range(nc):
                win = slice(C * j, C * (j + 2))
                sums = _dot(band, z_hi[win, cs]) + _dot(band, z_lo[win, cs])
                cnt = jnp.minimum(avail0 + C * j, w).astype(F32)
                means.append(sums / cnt - zc[C * j:C * (j + 1), cs])
            pooled.append(_dot(jnp.concatenate(means, axis=0).astype(BF16), wpool_ref[gi]))
        pool = jnp.concatenate(pooled, axis=1) * pscale_ref[...]
        z_scr[...] = zc[R - C:R, :]

        res = jnp.concatenate([gla, pool], axis=1).astype(BF16)
        mixed_ref[...] = res[0:rows_valid]

        @pl.when(blk == n_blk - 1)
        def _():
            sout_ref[...] = s_scr[...]
            hout_ref[...] = zc[rows_valid - HIST_ROWS:rows_valid, :]

    if fill_tail:
        pl.when(step < n_seq * n_blk)(process)

        @pl.when(step == n_seq * n_blk)
        def _():
            mixed_ref[...] = jnp.zeros(mixed_ref.shape, BF16)
    else:
        process()


def _mixer(qkvg, pa, wfu, bfg, gain, wpool, pscale, *, row0, n_seq, seq_len, n_hist, init=None, mixed_in=None):
    n = qkvg.shape[0]
    if seq_len % (2 * CHUNK) == 0:
        rb = _row_tile(seq_len, (512, 256, 128))
    else:
        rb = seq_len
        assert HIST_ROWS <= seq_len < CHUNK and init is not None
    assert seq_len % rb == 0 and row0 % rb == 0
    n_blk = seq_len // rb
    base = row0 // rb
    n_steps = n_seq * n_blk
    has_init = init is not None
    assert has_init == (mixed_in is not None)
    tail_rows = n - (row0 + n_seq * seq_len)
    fill_tail = (not has_init) and tail_rows > 0
    assert tail_rows <= rb

    rows = lambda t: (base + t, 0)
    const2 = lambda t: (0, 0)
    seq3 = lambda t: (jnp.minimum(t // n_blk, n_seq - 1), 0, 0)
    seq4 = lambda t: (jnp.minimum(t // n_blk, n_seq - 1), 0, 0, 0)

    in_specs = [
        pl.BlockSpec((rb, A_OFF), rows),
        pl.BlockSpec((rb, PA_COLS), rows),
        pl.BlockSpec((LANE, GLA_KW), const2),
        pl.BlockSpec((1, GLA_KW), const2),
        pl.BlockSpec((1, GLA_WIDTH), const2),
        pl.BlockSpec((POOL_GROUPS, POOL_GC, POOL_GC), lambda t: (0, 0, 0)),
        pl.BlockSpec((1, POOL_WIDTH), const2),
    ]
    args = [qkvg, pa, wfu, bfg, gain, wpool, pscale]
    aliases = {}
    if has_init:
        in_specs += [
            pl.BlockSpec((None, GLA_HEADS, GLA_DK, GLA_DV), seq4),
            pl.BlockSpec((None, HIST_ROWS, POOL_WIDTH), seq3),
            pl.BlockSpec(memory_space=pl.ANY),
        ]
        args += [init[0], init[1], mixed_in]
        aliases = {len(args) - 1: 0}
    return pl.pallas_call(
        functools.partial(_mixer_block_kernel, n_seq=n_seq, n_blk=n_blk, rows_valid=rb, n_hist=n_hist,
                          has_init=has_init, fill_tail=fill_tail),
        grid=(n_steps + int(fill_tail),),
        in_specs=in_specs,
        out_specs=[
            pl.BlockSpec((rb, D_MODEL), rows),
            pl.BlockSpec((None, GLA_HEADS, GLA_DK, GLA_DV), seq4),
            pl.BlockSpec((None, HIST_ROWS, POOL_WIDTH), seq3),
        ],
        out_shape=[
            jax.ShapeDtypeStruct((n, D_MODEL), BF16),
            jax.ShapeDtypeStruct((n_seq, GLA_HEADS, GLA_DK, GLA_DV), F32),
            jax.ShapeDtypeStruct((n_seq, HIST_ROWS, POOL_WIDTH), F32),
        ],
        scratch_shapes=[
            pltpu.VMEM((GLA_HEADS, GLA_DK, GLA_DV), F32),
            pltpu.VMEM((CHUNK, POOL_WIDTH), F32),
        ],
        input_output_aliases=aliases,
        compiler_params=_params(("arbitrary",)),
        name="mixer_init" if has_init else "mixer",
    )(*args)


def _route(logits):
    lane = lax.broadcasted_iota(jnp.int32, logits.shape, 1).astype(F32)
    neg = -jnp.inf
    first_lane = lambda hit: jnp.min(jnp.where(hit, lane, float(ROUTE_LANES)), axis=-1, keepdims=True)
    lg = jnp.where(lane < N_GROUPS, logits, neg)
    un = jnp.exp(lg - jnp.max(lg, axis=-1, keepdims=True))
    pg = un / jnp.sum(un, axis=-1, keepdims=True)
    top_pg = jnp.max(pg, axis=-1, keepdims=True)
    gsel = first_lane((pg == top_pg) & (lane < N_GROUPS))
    first = N_GROUPS + gsel * N_EXP
    le = jnp.where((lane >= first) & (lane < first + N_EXP), logits, neg)
    v0 = jnp.max(le, axis=-1, keepdims=True)
    i0 = first_lane(le == v0)
    le = jnp.where(lane == i0, neg, le)
    v1 = jnp.max(le, axis=-1, keepdims=True)
    i1 = first_lane(le == v1)
    u1 = jnp.exp(v1 - v0)
    den = 1.0 + u1
    e0 = (i0 - N_GROUPS).astype(jnp.int32)
    e1 = (i1 - N_GROUPS).astype(jnp.int32)
    return e0, e1, (1.0 / den) * top_pg, (u1 / den) * top_pg


def _out_proj_kernel(mixed_ref, x_ref, w_ref, g_ref, b_ref, wr_ref, br_ref, x1_ref, x1p_ref, ri_ref, rw_ref,
                     wrs_ref):
    tm = x_ref.shape[0]

    @pl.when(pl.program_id(0) == 0)
    def _():
        wr = wr_ref[...]
        wr_hi = wr.astype(BF16)
        wrs_ref[:, :ROUTE_LANES] = wr_hi
        wrs_ref[:, ROUTE_LANES:] = (wr - wr_hi.astype(F32)).astype(BF16)

    y = _dot(mixed_ref[...], w_ref[...])
    x1 = _layer_norm(DN_ALPHA * x_ref[...] + y, g_ref[...], b_ref[...])
    x1_ref[...] = x1
    x1p_ref[...] = _pack_pairs(x1)
    x_hi = x1.astype(BF16)
    x_lo = (x1 - x_hi.astype(F32)).astype(BF16)
    prod = _dot(jnp.concatenate([x_hi, x_lo], axis=0), wrs_ref[...])
    corr = prod[tm:, :ROUTE_LANES] + prod[:tm, ROUTE_LANES:]
    logits = prod[:tm, :ROUTE_LANES] + ROUTER_CORRECTION * corr + br_ref[...]
    e0, e1, w0, w1 = _route(logits)
    lane = lax.broadcasted_iota(jnp.int32, logits.shape, 1)
    ri_ref[...] = jnp.where(lane == 0, e0, jnp.where(lane == 1, e1, 0))
    rw_ref[...] = jnp.where(lane == 0, w0, jnp.where(lane == 1, w1, 0.0))


def _out_proj(mixed, x, w_out, g, b, wr, br, layer):
    n = x.shape[0]
    tm = _row_tile(n, (640, 416, 320, 256, 128))
    row = lambda i: (i, 0)
    const = lambda i: (0, 0)
    return pl.pallas_call(
        _out_proj_kernel,
        grid=(n // tm,),
        in_specs=[
            pl.BlockSpec((tm, D_MODEL), row),
            pl.BlockSpec((tm, D_MODEL), row),
            pl.BlockSpec((None, D_MODEL, D_MODEL), lambda i: (layer, 0, 0), pipeline_mode=pl.Buffered(1)),
            pl.BlockSpec((1, D_MODEL), const),
            pl.BlockSpec((1, D_MODEL), const),
            pl.BlockSpec((D_MODEL, ROUTE_LANES), const),
            pl.BlockSpec((1, ROUTE_LANES), const),
        ],
        out_specs=[
            pl.BlockSpec((tm, D_MODEL), row),
            pl.BlockSpec((tm, D_MODEL // 2), row),
            pl.BlockSpec((tm, ROUTE_LANES), row),
            pl.BlockSpec((tm, ROUTE_LANES), row),
        ],
        out_shape=[
            jax.ShapeDtypeStruct((n, D_MODEL), F32),
            jax.ShapeDtypeStruct((n, D_MODEL // 2), jnp.uint32),
            jax.ShapeDtypeStruct((n, ROUTE_LANES), jnp.int32),
            jax.ShapeDtypeStruct((n, ROUTE_LANES), F32),
        ],
        scratch_shapes=[pltpu.VMEM((D_MODEL, 2 * ROUTE_LANES), BF16)],
        compiler_params=_params(("arbitrary",)),
        name="out_proj_ln_route",
    )(mixed, x, w_out, g, b, wr, br)


def _dispatch_plan(eid):
    n_pairs = eid.shape[0] * 2
    n_tiles = n_pairs // EXPERT_TILE
    flat = eid.reshape(-1)
    onehot = (flat[:, None] == jnp.arange(N_EXPERTS, dtype=jnp.int32)[None, :]).astype(jnp.int32)
    csum = jnp.cumsum(onehot, axis=0)
    rank = jnp.sum(csum * onehot, axis=1) - 1
    counts = csum[-1]
    offs = jnp.concatenate([jnp.zeros((1,), jnp.int32), jnp.cumsum(counts)])
    pos = (offs[flat] + rank).astype(jnp.int32)
    bounds = jnp.sort(jnp.concatenate([jnp.arange(n_tiles, dtype=jnp.int32) * EXPERT_TILE, offs[:N_EXPERTS]]))
    seg_lo = bounds
    seg_hi = jnp.concatenate([bounds[1:], jnp.full((1,), n_pairs, jnp.int32)])
    tile = jnp.minimum(seg_lo // EXPERT_TILE, n_tiles - 1)
    expert = jnp.minimum(jnp.searchsorted(offs[1:], seg_lo, side="right"), N_EXPERTS - 1).astype(jnp.int32)
    lo = seg_lo - tile * EXPERT_TILE
    hi = jnp.where(seg_hi > seg_lo, seg_hi - tile * EXPERT_TILE, lo)
    valid = seg_hi > seg_lo
    e_seen = lax.cummax(jnp.where(valid, expert, -1))
    prev = jnp.concatenate([jnp.full((1,), -1, jnp.int32), e_seen[:-1]])
    fetch = valid & (expert > prev)
    run = jnp.maximum(jnp.cumsum(fetch.astype(jnp.int32)) - 1, 0)
    slot = run % 2
    fetch = fetch.astype(jnp.int32) * jnp.where(run == 0, 2, 1)
    ids = jnp.arange(N_EXPERTS, dtype=jnp.int32)
    later = lax.cummin(jnp.where(counts > 0, ids, N_EXPERTS), reverse=True)
    nxt_of = jnp.concatenate([later[1:], jnp.full((1,), N_EXPERTS, jnp.int32)])
    nxt = jnp.where(nxt_of[expert] < N_EXPERTS, nxt_of[expert], -1)
    sched = (fetch, slot.astype(jnp.int32), nxt.astype(jnp.int32))
    return pos, (tile.astype(jnp.int32), expert, lo.astype(jnp.int32), hi.astype(jnp.int32)) + sched


def _dispatch_kernel(pos_ref, x_ref, xs_hbm, sem, *, tb):
    base = pl.program_id(0) * tb

    def scatter(t, k):
        return pltpu.make_async_copy(x_ref.at[pl.ds(t, 1)], xs_hbm.at[pl.ds(pos_ref[2 * (base + t) + k], 1)], sem)

    def issue(t, c):
        scatter(t, 0).start()
        scatter(t, 1).start()
        return c

    lax.fori_loop(0, tb, issue, 0, unroll=8)

    def drain(t, c):
        scatter(t, 0).wait()
        scatter(t, 1).wait()
        return c

    lax.fori_loop(0, tb, drain, 0, unroll=8)


def _dispatch(pos, x1):
    n, width = x1.shape
    tb = _row_tile(n)
    return pl.pallas_call(
        functools.partial(_dispatch_kernel, tb=tb),
        grid_spec=pltpu.PrefetchScalarGridSpec(
            num_scalar_prefetch=1,
            grid=(n // tb,),
            in_specs=[pl.BlockSpec((tb, width), lambda i, p: (i, 0))],
            out_specs=pl.BlockSpec(memory_space=pl.ANY),
            scratch_shapes=[pltpu.SemaphoreType.DMA(())],
        ),
        out_shape=jax.ShapeDtypeStruct((2 * n, width), x1.dtype),
        compiler_params=_params(("arbitrary",)),
        name="moe_dispatch",
    )(pos, x1)


def _experts_kernel(tile_ref, exp_ref, lo_ref, hi_ref, fetch_ref, slot_ref, nxt_ref,
                    xs_ref, w1_hbm, w3_hbm, w2_hbm, o_ref, w1_buf, w3_buf, w2_buf, sem, *, layer):
    i = pl.program_id(0)
    lo, hi = lo_ref[i], hi_ref[i]
    slot = slot_ref[i]

    def weight_copies(expert, s):
        e = layer * N_EXPERTS + expert
        return (pltpu.make_async_copy(w1_hbm.at[e], w1_buf.at[s], sem.at[s, 0]),
                pltpu.make_async_copy(w3_hbm.at[e], w3_buf.at[s], sem.at[s, 1]),
                pltpu.make_async_copy(w2_hbm.at[e], w2_buf.at[s], sem.at[s, 2]))

    @pl.when(fetch_ref[i] == 2)
    def _():
        for c in weight_copies(exp_ref[i], slot):
            c.start()

    @pl.when(fetch_ref[i] > 0)
    def _():
        for c in weight_copies(exp_ref[i], slot):
            c.wait()

        @pl.when(nxt_ref[i] >= 0)
        def _():
            for c in weight_copies(nxt_ref[i], 1 - slot):
                c.start()

    def compute():
        x = _unpack_pairs(xs_ref[...]).astype(BF16)
        h1 = _dot(x, w1_buf[slot].astype(BF16))
        h3 = _dot(x, w3_buf[slot].astype(BF16))
        a = (h1 * jax.nn.sigmoid(h1) * h3).astype(BF16)
        return _pack_pairs(_dot(a, w2_buf[slot].astype(BF16)))

    @pl.when((hi > lo) & (lo == 0))
    def _():
        o_ref[...] = compute()

    @pl.when((hi > lo) & (lo > 0))
    def _():
        r = lax.broadcasted_iota(jnp.int32, o_ref.shape, 0)
        o_ref[...] = jnp.where((r >= lo) & (r < hi), compute(), o_ref[...])


def _experts(plan, xs, w1, w3, w2, layer):
    n_items = plan[0].shape[0]
    xrow = lambda i, t, *_: (t[i], 0)
    any_spec = pl.BlockSpec(memory_space=pl.ANY)
    return pl.pallas_call(
        functools.partial(_experts_kernel, layer=layer),
        grid_spec=pltpu.PrefetchScalarGridSpec(
            num_scalar_prefetch=len(plan),
            grid=(n_items,),
            in_specs=[pl.BlockSpec((EXPERT_TILE, D_MODEL // 2), xrow), any_spec, any_spec, any_spec],
            out_specs=pl.BlockSpec((EXPERT_TILE, D_MODEL // 2), xrow),
            scratch_shapes=[
                pltpu.VMEM((2, D_MODEL, EXPERT_HIDDEN), F32),
                pltpu.VMEM((2, D_MODEL, EXPERT_HIDDEN), F32),
                pltpu.VMEM((2, EXPERT_HIDDEN, D_MODEL), F32),
                pltpu.SemaphoreType.DMA((2, 3)),
            ],
        ),
        out_shape=jax.ShapeDtypeStruct(xs.shape, xs.dtype),
        compiler_params=_params(("arbitrary",)),
        name="moe_experts",
    )(*plan, xs, w1, w3, w2)


def _combine_kernel(pos_ref, x1_ref, rw_ref, ys_hbm, g_ref, b_ref, *rest, tm, n_prompt_blocks):
    if n_prompt_blocks is None:
        x2_ref, x2b_ref, buf, sem = rest
    else:
        yp_ref, ysm_ref, buf, sem = rest
    i = pl.program_id(0)
    base = i * tm

    def gather(t, k):
        src = ys_hbm.at[pl.ds(pos_ref[2 * (base + t) + k], 1)]
        return pltpu.make_async_copy(src, buf.at[k, pl.ds(t, 1)], sem)

    def issue(t, c):
        gather(t, 0).start()
        gather(t, 1).start()
        return c

    lax.fori_loop(0, tm, issue, 0, unroll=8)

    def drain(t, c):
        gather(t, 0).wait()
        gather(t, 1).wait()
        return c

    lax.fori_loop(0, tm, drain, 0, unroll=8)

    rw = rw_ref[...]
    f = rw[:, 0:1] * _unpack_pairs(buf[0]) + rw[:, 1:2] * _unpack_pairs(buf[1])
    x2 = _layer_norm(DN_ALPHA * x1_ref[...] + f, g_ref[...], b_ref[...])
    if n_prompt_blocks is None:
        x2_ref[...] = x2
        x2b_ref[...] = x2.astype(BF16)
    else:
        @pl.when(i < n_prompt_blocks)
        def _():
            yp_ref[...] = x2

        @pl.when(i >= n_prompt_blocks)
        def _():
            ysm_ref[...] = x2


def _combine(pos, x1, rw, ys, g, b, *, split=None):
    n = x1.shape[0]
    row = lambda i, p: (i, 0)
    const = lambda i, p: (0, 0)
    if split is None:
        tm, npb = _row_tile(n, (320, 256, 128)), None
        out_specs = [pl.BlockSpec((tm, D_MODEL), row), pl.BlockSpec((tm, D_MODEL), row)]
        out_shape = [jax.ShapeDtypeStruct((n, D_MODEL), F32), jax.ShapeDtypeStruct((n, D_MODEL), BF16)]
    else:
        n_p, n_s = split
        tm = n_s
        assert n_p % tm == 0 and n_p + n_s == n
        npb = n_p // tm
        out_specs = [
            pl.BlockSpec((tm, D_MODEL), lambda i, p: (jnp.minimum(i, npb - 1), 0)),
            pl.BlockSpec((tm, D_MODEL), lambda i, p: (jnp.maximum(i - npb, 0), 0)),
        ]
        out_shape = [jax.ShapeDtypeStruct((n_p, D_MODEL), F32), jax.ShapeDtypeStruct((n_s, D_MODEL), F32)]
    return pl.pallas_call(
        functools.partial(_combine_kernel, tm=tm, n_prompt_blocks=npb),
        grid_spec=pltpu.PrefetchScalarGridSpec(
            num_scalar_prefetch=1,
            grid=(n // tm,),
            in_specs=[
                pl.BlockSpec((tm, D_MODEL), row),
                pl.BlockSpec((tm, ROUTE_LANES), row),
                pl.BlockSpec(memory_space=pl.ANY),
                pl.BlockSpec((1, D_MODEL), const),
                pl.BlockSpec((1, D_MODEL), const),
            ],
            out_specs=out_specs,
            scratch_shapes=[pltpu.VMEM((2, tm, D_MODEL // 2), jnp.uint32), pltpu.SemaphoreType.DMA(())],
        ),
        out_shape=out_shape,
        compiler_params=_params(("arbitrary",)),
        name="moe_combine_ln" if split is None else "moe_combine_ln_final",
    )(pos, x1, rw, ys, g, b)


def kernel(x_prompt, x_sample, state_gla, cache_pool, ln_in_g, ln_in_b, w_in, w_forget_up, b_forget, gla_norm_g, w_pool, pool_scale, w_out, ln1_g, ln1_b, router_group_w, router_group_b, router_expert_w, router_expert_b, w_exp_gate, w_exp_up, w_exp_down, ln2_g, ln2_b):
    n_pb, seq, d = x_prompt.shape
    n_sb, dseq, _ = x_sample.shape
    n_p, n_s = n_pb * seq, n_sb * dseq
    depth = w_in.shape[0]
    row2 = lambda v: v.reshape(1, -1)

    w_out_b = w_out.astype(BF16)
    w_in_t = jnp.swapaxes(w_in, 1, 2)
    wfu = jnp.concatenate([w_forget_up, jnp.zeros((depth, LANE - GATE_RANK, GLA_KW), F32)], axis=1).astype(BF16)
    wpool = w_pool.astype(BF16)
    wr = jnp.concatenate(
        [router_group_w, router_expert_w.transpose(0, 2, 1, 3).reshape(depth, d, N_EXPERTS),
         jnp.zeros((depth, d, ROUTE_LANES - N_GROUPS - N_EXPERTS), F32)], axis=2)
    br = jnp.concatenate(
        [router_group_b, router_expert_b.reshape(depth, N_EXPERTS),
         jnp.zeros((depth, ROUTE_LANES - N_GROUPS - N_EXPERTS), F32)], axis=1)
    w1 = w_exp_gate.reshape(depth * N_EXPERTS, d, EXPERT_HIDDEN)
    w3 = w_exp_up.reshape(depth * N_EXPERTS, d, EXPERT_HIDDEN)
    w2 = w_exp_down.reshape(depth * N_EXPERTS, EXPERT_HIDDEN, d)
    hist0 = jnp.concatenate([jnp.zeros((depth, n_sb, 1, POOL_WIDTH), F32), cache_pool], axis=2)

    x, xb = _ln_in(x_prompt.reshape(n_p, d), x_sample.reshape(n_s, d), row2(ln_in_g), row2(ln_in_b))
    states_p, hists_p, states_s, hists_s = [], [], [], []
    for l in range(depth):
        qkvg = _in_proj_qkvg(xb, w_in_t, l)
        pa = _in_proj_pa(xb, w_in_t, l)
        mix_w = (wfu[l], row2(b_forget[l]), row2(gla_norm_g[l]), wpool[l], row2(pool_scale[l]))
        mixed, sp, hp = _mixer(qkvg, pa, *mix_w, row0=0, n_seq=n_pb, seq_len=seq, n_hist=0)
        mixed, ss, hs = _mixer(qkvg, pa, *mix_w, row0=n_p, n_seq=n_sb, seq_len=dseq, n_hist=POOL_HIST,
                               init=(state_gla[l], hist0[l]), mixed_in=mixed)
        x1, x1p, ri, rw = _out_proj(mixed, x, w_out_b, row2(ln1_g[l]), row2(ln1_b[l]), wr[l], row2(br[l]), l)
        pos, plan = _dispatch_plan(ri[:, :2])
        xs = _dispatch(pos, x1p)
        ys = _experts(plan, xs, w1, w3, w2, l)
        if l + 1 < depth:
            x, xb = _combine(pos, x1, rw, ys, row2(ln2_g[l]), row2(ln2_b[l]))
        else:
            y_p, y_s = _combine(pos, x1, rw, ys, row2(ln2_g[l]), row2(ln2_b[l]), split=(n_p, n_s))
        states_p.append(sp)
        hists_p.append(hp[:, 1:])
        states_s.append(ss)
        hists_s.append(hs[:, 1:])
    return (y_p.reshape(n_pb, seq, d), y_s.reshape(n_sb, dseq, d),
            jnp.stack(states_p), jnp.stack(hists_p), jnp.stack(states_s), jnp.stack(hists_s))
```

```python
import functools

import jax
import jax.numpy as jnp
from jax import lax
from jax.experimental import pallas as pl
from jax.experimental.pallas import tpu as pltpu

D_MODEL = 2048
DEPTH = 4
CHUNK = 64

GLA_HEADS = 4
GLA_WIDTH = D_MODEL // 2
GLA_DV = GLA_WIDTH // GLA_HEADS
GLA_DK = GLA_DV // 2
GLA_KW = GLA_HEADS * GLA_DK
GATE_RANK = 16
GATE_TAU = 16.0

POOL_WINDOWS = (2, 4, 8, 16)
POOL_GROUPS = len(POOL_WINDOWS)
POOL_WIDTH = D_MODEL - GLA_WIDTH
POOL_GC = POOL_WIDTH // POOL_GROUPS
POOL_HIST = max(POOL_WINDOWS) - 1
HIST_ROWS = POOL_HIST + 1

Q_OFF = 0
K_OFF = Q_OFF + GLA_KW
V_OFF = K_OFF + GLA_KW
G_OFF = V_OFF + GLA_WIDTH
A_OFF = G_OFF + GLA_WIDTH
P_OFF = A_OFF + GATE_RANK
IN_COLS = P_OFF + POOL_WIDTH
PA_COLS = POOL_WIDTH + 128

N_GROUPS = 4
N_EXP = 8
N_EXPERTS = N_GROUPS * N_EXP
EXPERT_HIDDEN = D_MODEL // 4
ROUTE_LANES = 128
ROUTER_CORRECTION = 0.5

DN_ALPHA = (2 * DEPTH) ** 0.25
LN_EPS = 1e-5
RMS_EPS = 1e-6

VMEM_LIMIT_BYTES = 56 * 1024 * 1024
LANE = 128
TOKEN_ROWS = D_MODEL // 2 // LANE
EXPERT_TILE = 256
BF16 = jnp.bfloat16
F32 = jnp.float32


def _params(sem, vmem=VMEM_LIMIT_BYTES):
    return pltpu.CompilerParams(dimension_semantics=sem, vmem_limit_bytes=vmem)


def _row_tile(n, prefs=(640, 512, 320, 256, 128)):
    for t in prefs:
        if n % t == 0:
            return t
    raise ValueError(f"no row tile for {n} rows")


def _layer_norm(x, g, b):
    mu = jnp.mean(x, axis=-1, keepdims=True)
    xc = x - mu
    var = jnp.mean(xc * xc, axis=-1, keepdims=True)
    return xc * lax.rsqrt(var + LN_EPS) * g + b


def _dot(a, b):
    return jnp.dot(a, b, preferred_element_type=F32)


def _dot_nt(a, b):
    return lax.dot_general(a, b, (((1,), (1,)), ((), ())), preferred_element_type=F32)


def _pack_pairs(x):
    k = x.shape[1] // 2
    hi = lax.bitcast_convert_type(x[:, :k].astype(BF16).astype(F32), jnp.uint32)
    lo = lax.bitcast_convert_type(x[:, k:].astype(BF16).astype(F32), jnp.uint32)
    return hi | (lo >> 16)


def _unpack_pairs(w):
    hi = lax.bitcast_convert_type(w & jnp.uint32(0xFFFF0000), F32)
    lo = lax.bitcast_convert_type(w << 16, F32)
    return jnp.concatenate([hi, lo], axis=1)


def _store_token_tiles(ref, words):
    m = words.shape[0]
    for j in range(TOKEN_ROWS):
        ref[pl.ds(j, m, stride=TOKEN_ROWS), :] = words[:, j * LANE:(j + 1) * LANE]


def _load_token_tiles(ref, m):
    return jnp.concatenate([ref[pl.ds(j, m, stride=TOKEN_ROWS), :] for j in range(TOKEN_ROWS)], axis=1)


def _dot_split(m_bf16, x):
    hi = x.astype(BF16)
    lo = (x - hi.astype(F32)).astype(BF16)
    return _dot(m_bf16, hi) + _dot(m_bf16, lo)


def _ln_in_kernel(xp_ref, xs_ref, g_ref, b_ref, x_ref, xb_ref, *, n_prompt_blocks):
    i = pl.program_id(0)
    x = jnp.where(i < n_prompt_blocks, xp_ref[...], xs_ref[...])
    y = _layer_norm(x, g_ref[...], b_ref[...])
    x_ref[...] = y
    xb_ref[...] = y.astype(BF16)


def _ln_in(xp, xs, g, b):
    n_p, n_s = xp.shape[0], xs.shape[0]
    tm = n_s
    assert n_p % tm == 0
    npb = n_p // tm
    n = n_p + n_s
    return pl.pallas_call(
        functools.partial(_ln_in_kernel, n_prompt_blocks=npb),
        grid=(npb + 1,),
        in_specs=[
            pl.BlockSpec((tm, D_MODEL), lambda i: (jnp.minimum(i, npb - 1), 0)),
            pl.BlockSpec((tm, D_MODEL), lambda i: (0, 0)),
            pl.BlockSpec((1, D_MODEL), lambda i: (0, 0)),
            pl.BlockSpec((1, D_MODEL), lambda i: (0, 0)),
        ],
        out_specs=[
            pl.BlockSpec((tm, D_MODEL), lambda i: (i, 0)),
            pl.BlockSpec((tm, D_MODEL), lambda i: (i, 0)),
        ],
        out_shape=[jax.ShapeDtypeStruct((n, D_MODEL), F32), jax.ShapeDtypeStruct((n, D_MODEL), BF16)],
        compiler_params=_params(("arbitrary",)),
        name="ln_in",
    )(xp, xs, g, b)


def _proj_qkvg_kernel(x_ref, wt_ref, o_ref, wb_ref):
    @pl.when(pl.program_id(1) == 0)
    def _():
        wb_ref[...] = wt_ref[...].astype(BF16)

    o_ref[...] = _dot_nt(x_ref[...], wb_ref[...]).astype(o_ref.dtype)


def _in_proj_qkvg(xb, w_in_t, layer):
    n = xb.shape[0]
    tm, tn = _row_tile(n), 1024
    return pl.pallas_call(
        _proj_qkvg_kernel,
        grid=(A_OFF // tn, n // tm),
        in_specs=[
            pl.BlockSpec((tm, D_MODEL), lambda j, m: (m, 0)),
            pl.BlockSpec((None, tn, D_MODEL), lambda j, m: (layer, j, 0)),
        ],
        out_specs=pl.BlockSpec((tm, tn), lambda j, m: (m, j)),
        out_shape=jax.ShapeDtypeStruct((n, A_OFF), BF16),
        scratch_shapes=[pltpu.VMEM((tn, D_MODEL), BF16)],
        compiler_params=_params(("arbitrary", "arbitrary")),
        name="in_proj_qkvg",
    )(xb, w_in_t)


def _proj_pa_kernel(x_ref, wt_hbm, o_ref, stage_ref, wb_ref, sem, *, layer):
    n_tail = IN_COLS - A_OFF

    @pl.when(pl.program_id(0) == 0)
    def _():
        copies = (
            pltpu.make_async_copy(wt_hbm.at[layer, pl.ds(P_OFF, POOL_WIDTH)], stage_ref.at[pl.ds(0, POOL_WIDTH)],
                                  sem.at[0]),
            pltpu.make_async_copy(wt_hbm.at[layer, pl.ds(A_OFF, GATE_RANK)],
                                  stage_ref.at[pl.ds(POOL_WIDTH, GATE_RANK)], sem.at[1]),
        )
        for c in copies:
            c.start()
        for c in copies:
            c.wait()
        wb_ref[0:n_tail, :] = stage_ref[...].astype(BF16)
        wb_ref[n_tail:, :] = jnp.zeros((PA_COLS - n_tail, D_MODEL), BF16)

    o_ref[...] = _dot_nt(x_ref[...], wb_ref[...])


def _in_proj_pa(xb, w_in_t, layer):
    n = xb.shape[0]
    tm = _row_tile(n)
    return pl.pallas_call(
        functools.partial(_proj_pa_kernel, layer=layer),
        grid=(n // tm,),
        in_specs=[
            pl.BlockSpec((tm, D_MODEL), lambda m: (m, 0)),
            pl.BlockSpec(memory_space=pl.ANY),
        ],
        out_specs=pl.BlockSpec((tm, PA_COLS), lambda m: (m, 0)),
        out_shape=jax.ShapeDtypeStruct((n, PA_COLS), F32),
        scratch_shapes=[
            pltpu.VMEM((IN_COLS - A_OFF, D_MODEL), F32),
            pltpu.VMEM((PA_COLS, D_MODEL), BF16),
            pltpu.SemaphoreType.DMA((2,)),
        ],
        compiler_params=_params(("arbitrary",)),
        name="in_proj_pa",
    )(xb, w_in_t)


def _mixer_kernel(*refs, n_seq, n_blk, n_chunks, valid, n_hist, has_init, fill_tail):
    if has_init:
        (qkvg_ref, pa_ref, wfu_ref, bf_ref, gain_ref, wpool_ref, pscale_ref, s0_ref, h0_ref, _mixed_in,
         mixed_ref, sout_ref, hout_ref, s_scr, z_scr) = refs
    else:
        (qkvg_ref, pa_ref, wfu_ref, bf_ref, gain_ref, wpool_ref, pscale_ref,
         mixed_ref, sout_ref, hout_ref, s_scr, z_scr) = refs
    C = CHUNK
    step = pl.program_id(0)
    blk = step % n_blk

    def pad_rows(x):
        if valid == C:
            return x
        return jnp.concatenate([x, jnp.zeros((C - valid, x.shape[1]), x.dtype)], axis=0)

    def process():
        @pl.when(blk == 0)
        def _():
            if has_init:
                s_scr[...] = s0_ref[...]
                z_scr[0:C - HIST_ROWS, :] = jnp.zeros((C - HIST_ROWS, POOL_WIDTH), F32)
                z_scr[C - HIST_ROWS:C, :] = h0_ref[...]
            else:
                s_scr[...] = jnp.zeros(s_scr.shape, F32)
                z_scr[...] = jnp.zeros(z_scr.shape, F32)

        row = lax.broadcasted_iota(jnp.int32, (C, C), 0)
        col = lax.broadcasted_iota(jnp.int32, (C, C), 1)
        causal = row >= col
        tri = causal.astype(BF16)
        t_idx = lax.broadcasted_iota(jnp.int32, (C, 2 * C), 0)
        s_idx = lax.broadcasted_iota(jnp.int32, (C, 2 * C), 1) - C
        lag = t_idx - s_idx
        bands = [((lag >= 0) & (lag < w)).astype(BF16) for w in POOL_WINDOWS]

        def chunk(j, carry):
            if n_chunks == 1:
                rows = slice(0, valid)
            else:
                rows = pl.ds(pl.multiple_of(j * C, C), C)
            qkvg = pad_rows(qkvg_ref[rows, :])
            pa = pad_rows(pa_ref[rows, :])
            q = qkvg[:, Q_OFF:K_OFF].astype(F32)
            k = qkvg[:, K_OFF:V_OFF].astype(F32)
            v = qkvg[:, V_OFF:G_OFF]
            g = qkvg[:, G_OFF:A_OFF].astype(F32)
            zc = pa[:, :POOL_WIDTH]
            a = pa[:, POOL_WIDTH:]

            la = jax.nn.log_sigmoid(_dot(a.astype(BF16), wfu_ref[...]) + bf_ref[...]) * (1.0 / GATE_TAU)
            if valid != C:
                la = jnp.where(lax.broadcasted_iota(jnp.int32, la.shape, 0) < valid, la, 0.0)
            bcum = _dot_split(tri, la)
            b_last = bcum[C - 1:C, :]
            q_t = (q * jnp.exp(bcum) * (GLA_DK ** -0.5)).astype(BF16)
            k_t = (k * jnp.exp(-bcum)).astype(BF16)
            k_dec = k * jnp.exp(b_last - bcum)
            e_last = jnp.exp(b_last)
            outs = []
            for h in range(GLA_HEADS):
                ks = slice(h * GLA_DK, (h + 1) * GLA_DK)
                vs = slice(h * GLA_DV, (h + 1) * GLA_DV)
                s_h = s_scr[h]
                att = jnp.where(causal, _dot_nt(q_t[:, ks], k_t[:, ks]), 0.0)
                o = _dot(att.astype(BF16), v[:, vs]) + _dot(q_t[:, ks], s_h.astype(BF16))
                k_aug = jnp.concatenate([k_dec[:, ks], jnp.zeros((GLA_DK - C, GLA_DK), F32)], axis=0)
                v_aug = jnp.concatenate([v[:, vs], jnp.zeros((GLA_DK - C, GLA_DV), BF16)], axis=0)
                decay = jnp.broadcast_to(e_last[:, ks], (GLA_DK, GLA_DK)).T
                decay = jnp.concatenate([decay] * (GLA_DV // GLA_DK), axis=1)
                s_scr[h] = decay * s_h + _dot(k_aug.T.astype(BF16), v_aug)
                outs.append(o * lax.rsqrt(jnp.mean(o * o, axis=-1, keepdims=True) + RMS_EPS))
            gla = jnp.concatenate(outs, axis=1) * gain_ref[...] * (g * jax.nn.sigmoid(g))

            z_ext = jnp.concatenate([z_scr[...], zc], axis=0)
            t0 = (blk * n_chunks + j) * C + n_hist + 1
            avail = lax.broadcasted_iota(jnp.int32, (C, POOL_GC), 0) + t0
            pooled = []
            for gi, w in enumerate(POOL_WINDOWS):
                cs = slice(gi * POOL_GC, (gi + 1) * POOL_GC)
                cnt = jnp.minimum(avail, w).astype(F32)
                m = _dot_split(bands[gi], z_ext[:, cs]) / cnt - zc[:, cs]
                pooled.append(_dot(m.astype(BF16), wpool_ref[gi]))
            pool = jnp.concatenate(pooled, axis=1) * pscale_ref[...]
            z_scr[...] = zc

            res = jnp.concatenate([gla, pool], axis=1).astype(BF16)
            mixed_ref[rows, :] = res[0:valid]
            return carry

        if n_chunks == 1:
            chunk(0, 0)
        else:
            lax.fori_loop(0, n_chunks, chunk, 0, unroll=2)

        @pl.when(blk == n_blk - 1)
        def _():
            sout_ref[...] = s_scr[...]
            hout_ref[...] = z_scr[valid - HIST_ROWS:valid, :]

    if fill_tail:
        pl.when(step < n_seq * n_blk)(process)

        @pl.when(step == n_seq * n_blk)
        def _():
            mixed_ref[...] = jnp.zeros(mixed_ref.shape, BF16)
    else:
        process()


def _mixer_block_kernel(*refs, n_seq, n_blk, rows_valid, n_hist, has_init, fill_tail):
    if has_init:
        (qkvg_ref, pa_ref, wfu_ref, bf_ref, gain_ref, wpool_ref, pscale_ref, s0_ref, h0_ref, _mixed_in,
         mixed_ref, sout_ref, hout_ref, s_scr, z_scr) = refs
    else:
        (qkvg_ref, pa_ref, wfu_ref, bf_ref, gain_ref, wpool_ref, pscale_ref,
         mixed_ref, sout_ref, hout_ref, s_scr, z_scr) = refs
    C = CHUNK
    R = max(rows_valid, 2 * C)
    nc = R // C
    step = pl.program_id(0)
    blk = step % n_blk

    def pad_rows(x):
        if rows_valid == R:
            return x
        return jnp.concatenate([x, jnp.zeros((R - rows_valid, x.shape[1]), x.dtype)], axis=0)

    def process():
        @pl.when(blk == 0)
        def _():
            if has_init:
                s_scr[...] = s0_ref[...]
                z_scr[0:C - HIST_ROWS, :] = jnp.zeros((C - HIST_ROWS, POOL_WIDTH), F32)
                z_scr[C - HIST_ROWS:C, :] = h0_ref[...]
            else:
                s_scr[...] = jnp.zeros(s_scr.shape, F32)
                z_scr[...] = jnp.zeros(z_scr.shape, F32)

        qkvg = pad_rows(qkvg_ref[...])
        pa = pad_rows(pa_ref[...])
        q = qkvg[:, Q_OFF:K_OFF].astype(F32)
        k = qkvg[:, K_OFF:V_OFF].astype(F32)
        v = qkvg[:, V_OFF:G_OFF]
        g = qkvg[:, G_OFF:A_OFF].astype(F32)
        zc = pa[:, :POOL_WIDTH]
        a = pa[:, POOL_WIDTH:]

        row = lax.broadcasted_iota(jnp.int32, (R, R), 0)
        col = lax.broadcasted_iota(jnp.int32, (R, R), 1)
        causal = (row // C == col // C) & (row >= col)
        la = jax.nn.log_sigmoid(_dot(a.astype(BF16), wfu_ref[...]) + bf_ref[...]) * (1.0 / GATE_TAU)
        if rows_valid != R:
            la = jnp.where(lax.broadcasted_iota(jnp.int32, la.shape, 0) < rows_valid, la, 0.0)
        bcum = _dot_split(causal.astype(BF16), la)
        tot = bcum.reshape(nc, C, GLA_KW)[:, C - 1:C, :]
        tot = jnp.broadcast_to(tot, (nc, C, GLA_KW)).reshape(R, GLA_KW)
        q_t = (q * jnp.exp(bcum) * (GLA_DK ** -0.5)).astype(BF16)
        k_t = (k * jnp.exp(-bcum)).astype(BF16)
        k_dec = k * jnp.exp(tot - bcum)
        e_tot = jnp.exp(tot)
        first_half = lax.broadcasted_iota(jnp.int32, (GLA_DK, 2 * C), 1) < C
        outs = []
        for h in range(GLA_HEADS):
            ks = slice(h * GLA_DK, (h + 1) * GLA_DK)
            vs = slice(h * GLA_DV, (h + 1) * GLA_DV)
            att = jnp.where(causal, _dot_nt(q_t[:, ks], k_t[:, ks]), 0.0)
            o_intra = _dot(att.astype(BF16), v[:, vs])
            k_dec_t = k_dec[:, ks].T
            e_tot_t = e_tot[:, ks].T
            updates = []
            for m in range(nc // 2):
                kt_pair = k_dec_t[:, 2 * C * m:2 * C * (m + 1)]
                v_pair = v[2 * C * m:2 * C * (m + 1), vs]
                updates.append(_dot(jnp.where(first_half, kt_pair, 0.0).astype(BF16), v_pair))
                updates.append(_dot(jnp.where(first_half, 0.0, kt_pair).astype(BF16), v_pair))
            s_h = s_scr[h]
            o_inter = []
            for j in range(nc):
                o_inter.append(_dot(q_t[C * j:C * (j + 1), ks], s_h.astype(BF16)))
                decay = jnp.broadcast_to(e_tot_t[:, C * j:C * j + 1], (GLA_DK, GLA_DV))
                s_h = decay * s_h + updates[j]
            s_scr[h] = s_h
            o = o_intra + jnp.concatenate(o_inter, axis=0)
            outs.append(o * lax.rsqrt(jnp.mean(o * o, axis=-1, keepdims=True) + RMS_EPS))
        gla = jnp.concatenate(outs, axis=1) * gain_ref[...] * (g * jax.nn.sigmoid(g))

        z_all = jnp.concatenate([z_scr[...], zc], axis=0)
        z_hi = z_all.astype(BF16)
        z_lo = (z_all - z_hi.astype(F32)).astype(BF16)
        t_idx = lax.broadcasted_iota(jnp.int32, (C, 2 * C), 0)
        s_idx = lax.broadcasted_iota(jnp.int32, (C, 2 * C), 1) - C
        lag = t_idx - s_idx
        avail0 = lax.broadcasted_iota(jnp.int32, (C, POOL_GC), 0) + (blk * rows_valid + n_hist + 1)
        pooled = []
        for gi, w in enumerate(POOL_WINDOWS):
            cs = slice(gi * POOL_GC, (gi + 1) * POOL_GC)
            band = ((lag >= 0) & (lag < w)).astype(BF16)
            means = []
            for j in range(nc):
                win = slice(C * j, C * (j + 2))
                sums = _dot(band, z_hi[win, cs]) + _dot(band, z_lo[win, cs])
                cnt = jnp.minimum(avail0 + C * j, w).astype(F32)
                means.append(sums / cnt - zc[C * j:C * (j + 1), cs])
            pooled.append(_dot(jnp.concatenate(means, axis=0).astype(BF16), wpool_ref[gi]))
        pool = jnp.concatenate(pooled, axis=1) * pscale_ref[...]
        z_scr[...] = zc[R - C:R, :]

        res = jnp.concatenate([gla, pool], axis=1).astype(BF16)
        mixed_ref[...] = res[0:rows_valid]

        @pl.when(blk == n_blk - 1)
        def _():
            sout_ref[...] = s_scr[...]
            hout_ref[...] = zc[rows_valid - HIST_ROWS:rows_valid, :]

    if fill_tail:
        pl.when(step < n_seq * n_blk)(process)

        @pl.when(step == n_seq * n_blk)
        def _():
            mixed_ref[...] = jnp.zeros(mixed_ref.shape, BF16)
    else:
        process()


def _mixer(qkvg, pa, wfu, bfg, gain, wpool, pscale, *, row0, n_seq, seq_len, n_hist, init=None, mixed_in=None):
    n = qkvg.shape[0]
    if seq_len % (2 * CHUNK) == 0:
        rb = _row_tile(seq_len, (512, 256, 128))
    else:
        rb = seq_len
        assert HIST_ROWS <= seq_len < CHUNK and init is not None
    assert seq_len % rb == 0 and row0 % rb == 0
    n_blk = seq_len // rb
    base = row0 // rb
    n_steps = n_seq * n_blk
    has_init = init is not None
    assert has_init == (mixed_in is not None)
    tail_rows = n - (row0 + n_seq * seq_len)
    fill_tail = (not has_init) and tail_rows > 0
    assert tail_rows <= rb

    rows = lambda t: (base + t, 0)
    const2 = lambda t: (0, 0)
    seq3 = lambda t: (jnp.minimum(t // n_blk, n_seq - 1), 0, 0)
    seq4 = lambda t: (jnp.minimum(t // n_blk, n_seq - 1), 0, 0, 0)

    in_specs = [
        pl.BlockSpec((rb, A_OFF), rows),
        pl.BlockSpec((rb, PA_COLS), rows),
        pl.BlockSpec((LANE, GLA_KW), const2),
        pl.BlockSpec((1, GLA_KW), const2),
        pl.BlockSpec((1, GLA_WIDTH), const2),
        pl.BlockSpec((POOL_GROUPS, POOL_GC, POOL_GC), lambda t: (0, 0, 0)),
        pl.BlockSpec((1, POOL_WIDTH), const2),
    ]
    args = [qkvg, pa, wfu, bfg, gain, wpool, pscale]
    aliases = {}
    if has_init:
        in_specs += [
            pl.BlockSpec((None, GLA_HEADS, GLA_DK, GLA_DV), seq4),
            pl.BlockSpec((None, HIST_ROWS, POOL_WIDTH), seq3),
            pl.BlockSpec(memory_space=pl.ANY),
        ]
        args += [init[0], init[1], mixed_in]
        aliases = {len(args) - 1: 0}
    return pl.pallas_call(
        functools.partial(_mixer_block_kernel, n_seq=n_seq, n_blk=n_blk, rows_valid=rb, n_hist=n_hist,
                          has_init=has_init, fill_tail=fill_tail),
        grid=(n_steps + int(fill_tail),),
        in_specs=in_specs,
        out_specs=[
            pl.BlockSpec((rb, D_MODEL), rows),
            pl.BlockSpec((None, GLA_HEADS, GLA_DK, GLA_DV), seq4),
            pl.BlockSpec((None, HIST_ROWS, POOL_WIDTH), seq3),
        ],
        out_shape=[
            jax.ShapeDtypeStruct((n, D_MODEL), BF16),
            jax.ShapeDtypeStruct((n_seq, GLA_HEADS, GLA_DK, GLA_DV), F32),
            jax.ShapeDtypeStruct((n_seq, HIST_ROWS, POOL_WIDTH), F32),
        ],
        scratch_shapes=[
            pltpu.VMEM((GLA_HEADS, GLA_DK, GLA_DV), F32),
            pltpu.VMEM((CHUNK, POOL_WIDTH), F32),
        ],
        input_output_aliases=aliases,
        compiler_params=_params(("arbitrary",)),
        name="mixer_init" if has_init else "mixer",
    )(*args)


def _route(logits):
    lane = lax.broadcasted_iota(jnp.int32, logits.shape, 1).astype(F32)
    neg = -jnp.inf
    first_lane = lambda hit: jnp.min(jnp.where(hit, lane, float(ROUTE_LANES)), axis=-1, keepdims=True)
    lg = jnp.where(lane < N_GROUPS, logits, neg)
    un = jnp.exp(lg - jnp.max(lg, axis=-1, keepdims=True))
    pg = un / jnp.sum(un, axis=-1, keepdims=True)
    top_pg = jnp.max(pg, axis=-1, keepdims=True)
    gsel = first_lane((pg == top_pg) & (lane < N_GROUPS))
    first = N_GROUPS + gsel * N_EXP
    le = jnp.where((lane >= first) & (lane < first + N_EXP), logits, neg)
    v0 = jnp.max(le, axis=-1, keepdims=True)
    i0 = first_lane(le == v0)
    le = jnp.where(lane == i0, neg, le)
    v1 = jnp.max(le, axis=-1, keepdims=True)
    i1 = first_lane(le == v1)
    u1 = jnp.exp(v1 - v0)
    den = 1.0 + u1
    e0 = (i0 - N_GROUPS).astype(jnp.int32)
    e1 = (i1 - N_GROUPS).astype(jnp.int32)
    return e0, e1, (1.0 / den) * top_pg, (u1 / den) * top_pg


def _out_proj_kernel(mixed_ref, x_ref, w_ref, g_ref, b_ref, wr_ref, br_ref, x1_ref, x1p_ref, ri_ref, rw_ref,
                     wrs_ref):
    tm = x_ref.shape[0]

    @pl.when(pl.program_id(0) == 0)
    def _():
        wr = wr_ref[...]
        wr_hi = wr.astype(BF16)
        wrs_ref[:, :ROUTE_LANES] = wr_hi
        wrs_ref[:, ROUTE_LANES:] = (wr - wr_hi.astype(F32)).astype(BF16)

    y = _dot(mixed_ref[...], w_ref[...])
    x1 = _layer_norm(DN_ALPHA * x_ref[...] + y, g_ref[...], b_ref[...])
    x1_ref[...] = x1
    _store_token_tiles(x1p_ref, _pack_pairs(x1))
    x_hi = x1.astype(BF16)
    x_lo = (x1 - x_hi.astype(F32)).astype(BF16)
    prod = _dot(jnp.concatenate([x_hi, x_lo], axis=0), wrs_ref[...])
    corr = prod[tm:, :ROUTE_LANES] + prod[:tm, ROUTE_LANES:]
    logits = prod[:tm, :ROUTE_LANES] + ROUTER_CORRECTION * corr + br_ref[...]
    e0, e1, w0, w1 = _route(logits)
    lane = lax.broadcasted_iota(jnp.int32, logits.shape, 1)
    ri_ref[...] = jnp.where(lane == 0, e0, jnp.where(lane == 1, e1, 0))
    rw_ref[...] = jnp.where(lane == 0, w0, jnp.where(lane == 1, w1, 0.0))


def _out_proj(mixed, x, w_out, g, b, wr, br, layer):
    n = x.shape[0]
    tm = _row_tile(n, (640, 416, 320, 256, 128))
    row = lambda i: (i, 0)
    const = lambda i: (0, 0)
    return pl.pallas_call(
        _out_proj_kernel,
        grid=(n // tm,),
        in_specs=[
            pl.BlockSpec((tm, D_MODEL), row),
            pl.BlockSpec((tm, D_MODEL), row),
            pl.BlockSpec((None, D_MODEL, D_MODEL), lambda i: (layer, 0, 0), pipeline_mode=pl.Buffered(1)),
            pl.BlockSpec((1, D_MODEL), const),
            pl.BlockSpec((1, D_MODEL), const),
            pl.BlockSpec((D_MODEL, ROUTE_LANES), const),
            pl.BlockSpec((1, ROUTE_LANES), const),
        ],
        out_specs=[
            pl.BlockSpec((tm, D_MODEL), row),
            pl.BlockSpec((tm * TOKEN_ROWS, LANE), row),
            pl.BlockSpec((tm, ROUTE_LANES), row),
            pl.BlockSpec((tm, ROUTE_LANES), row),
        ],
        out_shape=[
            jax.ShapeDtypeStruct((n, D_MODEL), F32),
            jax.ShapeDtypeStruct((n * TOKEN_ROWS, LANE), jnp.uint32),
            jax.ShapeDtypeStruct((n, ROUTE_LANES), jnp.int32),
            jax.ShapeDtypeStruct((n, ROUTE_LANES), F32),
        ],
        scratch_shapes=[pltpu.VMEM((D_MODEL, 2 * ROUTE_LANES), BF16)],
        compiler_params=_params(("arbitrary",)),
        name="out_proj_ln_route",
    )(mixed, x, w_out, g, b, wr, br)


def _dispatch_plan(eid):
    n_pairs = eid.shape[0] * 2
    n_tiles = n_pairs // EXPERT_TILE
    flat = eid.reshape(-1)
    onehot = (flat[:, None] == jnp.arange(N_EXPERTS, dtype=jnp.int32)[None, :]).astype(jnp.int32)
    csum = jnp.cumsum(onehot, axis=0)
    rank = jnp.sum(csum * onehot, axis=1) - 1
    counts = csum[-1]
    offs = jnp.concatenate([jnp.zeros((1,), jnp.int32), jnp.cumsum(counts)])
    pos = (offs[flat] + rank).astype(jnp.int32)
    bounds = jnp.sort(jnp.concatenate([jnp.arange(n_tiles, dtype=jnp.int32) * EXPERT_TILE, offs[:N_EXPERTS]]))
    seg_lo = bounds
    seg_hi = jnp.concatenate([bounds[1:], jnp.full((1,), n_pairs, jnp.int32)])
    tile = jnp.minimum(seg_lo // EXPERT_TILE, n_tiles - 1)
    expert = jnp.minimum(jnp.searchsorted(offs[1:], seg_lo, side="right"), N_EXPERTS - 1).astype(jnp.int32)
    lo = seg_lo - tile * EXPERT_TILE
    hi = jnp.where(seg_hi > seg_lo, seg_hi - tile * EXPERT_TILE, lo)
    valid = seg_hi > seg_lo
    e_seen = lax.cummax(jnp.where(valid, expert, -1))
    prev = jnp.concatenate([jnp.full((1,), -1, jnp.int32), e_seen[:-1]])
    fetch = valid & (expert > prev)
    run = jnp.maximum(jnp.cumsum(fetch.astype(jnp.int32)) - 1, 0)
    slot = run % 2
    fetch = fetch.astype(jnp.int32) * jnp.where(run == 0, 2, 1)
    ids = jnp.arange(N_EXPERTS, dtype=jnp.int32)
    later = lax.cummin(jnp.where(counts > 0, ids, N_EXPERTS), reverse=True)
    nxt_of = jnp.concatenate([later[1:], jnp.full((1,), N_EXPERTS, jnp.int32)])
    nxt = jnp.where(nxt_of[expert] < N_EXPERTS, nxt_of[expert], -1)
    sched = (fetch, slot.astype(jnp.int32), nxt.astype(jnp.int32))
    return pos, (tile.astype(jnp.int32), expert, lo.astype(jnp.int32), hi.astype(jnp.int32)) + sched


def _dispatch_kernel(pos_ref, x_ref, xs_hbm, sem, *, tb):
    base = pl.program_id(0) * tb

    def scatter(t, k):
        dst = pl.multiple_of(pos_ref[2 * (base + t) + k] * TOKEN_ROWS, TOKEN_ROWS)
        src = pl.multiple_of(t * TOKEN_ROWS, TOKEN_ROWS)
        return pltpu.make_async_copy(x_ref.at[pl.ds(src, TOKEN_ROWS)], xs_hbm.at[pl.ds(dst, TOKEN_ROWS)], sem)

    def issue(t, c):
        scatter(t, 0).start()
        scatter(t, 1).start()
        return c

    lax.fori_loop(0, tb, issue, 0, unroll=8)

    def drain(t, c):
        scatter(t, 0).wait()
        scatter(t, 1).wait()
        return c

    lax.fori_loop(0, tb, drain, 0, unroll=8)


def _dispatch(pos, x1):
    n = x1.shape[0] // TOKEN_ROWS
    tb = _row_tile(n)
    return pl.pallas_call(
        functools.partial(_dispatch_kernel, tb=tb),
        grid_spec=pltpu.PrefetchScalarGridSpec(
            num_scalar_prefetch=1,
            grid=(n // tb,),
            in_specs=[pl.BlockSpec((tb * TOKEN_ROWS, LANE), lambda i, p: (i, 0))],
            out_specs=pl.BlockSpec(memory_space=pl.ANY),
            scratch_shapes=[pltpu.SemaphoreType.DMA(())],
        ),
        out_shape=jax.ShapeDtypeStruct((2 * n * TOKEN_ROWS, LANE), x1.dtype),
        compiler_params=_params(("arbitrary",)),
        name="moe_dispatch",
    )(pos, x1)


def _experts_kernel(tile_ref, exp_ref, lo_ref, hi_ref, fetch_ref, slot_ref, nxt_ref,
                    xs_ref, w1_hbm, w3_hbm, w2_hbm, o_ref, w1_buf, w3_buf, w2_buf, sem, *, layer):
    i = pl.program_id(0)
    lo, hi = lo_ref[i], hi_ref[i]
    slot = slot_ref[i]

    def weight_copies(expert, s):
        e = layer * N_EXPERTS + expert
        return (pltpu.make_async_copy(w1_hbm.at[e], w1_buf.at[s], sem.at[s, 0]),
                pltpu.make_async_copy(w3_hbm.at[e], w3_buf.at[s], sem.at[s, 1]),
                pltpu.make_async_copy(w2_hbm.at[e], w2_buf.at[s], sem.at[s, 2]))

    @pl.when(fetch_ref[i] == 2)
    def _():
        for c in weight_copies(exp_ref[i], slot):
            c.start()

    @pl.when(fetch_ref[i] > 0)
    def _():
        for c in weight_copies(exp_ref[i], slot):
            c.wait()

        @pl.when(nxt_ref[i] >= 0)
        def _():
            for c in weight_copies(nxt_ref[i], 1 - slot):
                c.start()

    def compute():
        x = _unpack_pairs(_load_token_tiles(xs_ref, EXPERT_TILE)).astype(BF16)
        h1 = _dot(x, w1_buf[slot].astype(BF16))
        h3 = _dot(x, w3_buf[slot].astype(BF16))
        a = (h1 * jax.nn.sigmoid(h1) * h3).astype(BF16)
        return _pack_pairs(_dot(a, w2_buf[slot].astype(BF16)))

    @pl.when((hi > lo) & (lo == 0))
    def _():
        _store_token_tiles(o_ref, compute())

    @pl.when((hi > lo) & (lo > 0))
    def _():
        r = lax.broadcasted_iota(jnp.int32, (EXPERT_TILE, D_MODEL // 2), 0)
        old = _load_token_tiles(o_ref, EXPERT_TILE)
        _store_token_tiles(o_ref, jnp.where((r >= lo) & (r < hi), compute(), old))


def _experts(plan, xs, w1, w3, w2, layer):
    n_items = plan[0].shape[0]
    xrow = lambda i, t, *_: (t[i], 0)
    any_spec = pl.BlockSpec(memory_space=pl.ANY)
    return pl.pallas_call(
        functools.partial(_experts_kernel, layer=layer),
        grid_spec=pltpu.PrefetchScalarGridSpec(
            num_scalar_prefetch=len(plan),
            grid=(n_items,),
            in_specs=[pl.BlockSpec((EXPERT_TILE * TOKEN_ROWS, LANE), xrow), any_spec, any_spec, any_spec],
            out_specs=pl.BlockSpec((EXPERT_TILE * TOKEN_ROWS, LANE), xrow),
            scratch_shapes=[
                pltpu.VMEM((2, D_MODEL, EXPERT_HIDDEN), F32),
                pltpu.VMEM((2, D_MODEL, EXPERT_HIDDEN), F32),
                pltpu.VMEM((2, EXPERT_HIDDEN, D_MODEL), F32),
                pltpu.SemaphoreType.DMA((2, 3)),
            ],
        ),
        out_shape=jax.ShapeDtypeStruct(xs.shape, xs.dtype),
        compiler_params=_params(("arbitrary",)),
        name="moe_experts",
    )(*plan, xs, w1, w3, w2)


def _combine_kernel(pos_ref, x1_ref, rw_ref, ys_hbm, g_ref, b_ref, *rest, tm, n_prompt_blocks):
    if n_prompt_blocks is None:
        x2_ref, x2b_ref, buf, sem = rest
    else:
        yp_ref, ysm_ref, buf, sem = rest
    i = pl.program_id(0)
    base = i * tm

    def gather(t, k):
        src = pl.multiple_of(pos_ref[2 * (base + t) + k] * TOKEN_ROWS, TOKEN_ROWS)
        dst = pl.multiple_of(t * TOKEN_ROWS, TOKEN_ROWS)
        return pltpu.make_async_copy(ys_hbm.at[pl.ds(src, TOKEN_ROWS)], buf.at[k, pl.ds(dst, TOKEN_ROWS)], sem)

    def issue(t, c):
        gather(t, 0).start()
        gather(t, 1).start()
        return c

    lax.fori_loop(0, tm, issue, 0, unroll=8)

    def drain(t, c):
        gather(t, 0).wait()
        gather(t, 1).wait()
        return c

    lax.fori_loop(0, tm, drain, 0, unroll=8)

    rw = rw_ref[...]
    f = (rw[:, 0:1] * _unpack_pairs(_load_token_tiles(buf.at[0], tm))
         + rw[:, 1:2] * _unpack_pairs(_load_token_tiles(buf.at[1], tm)))
    x2 = _layer_norm(DN_ALPHA * x1_ref[...] + f, g_ref[...], b_ref[...])
    if n_prompt_blocks is None:
        x2_ref[...] = x2
        x2b_ref[...] = x2.astype(BF16)
    else:
        @pl.when(i < n_prompt_blocks)
        def _():
            yp_ref[...] = x2

        @pl.when(i >= n_prompt_blocks)
        def _():
            ysm_ref[...] = x2


def _combine(pos, x1, rw, ys, g, b, *, split=None):
    n = x1.shape[0]
    row = lambda i, p: (i, 0)
    const = lambda i, p: (0, 0)
    if split is None:
        tm, npb = _row_tile(n, (320, 256, 128)), None
        out_specs = [pl.BlockSpec((tm, D_MODEL), row), pl.BlockSpec((tm, D_MODEL), row)]
        out_shape = [jax.ShapeDtypeStruct((n, D_MODEL), F32), jax.ShapeDtypeStruct((n, D_MODEL), BF16)]
    else:
        n_p, n_s = split
        tm = n_s
        assert n_p % tm == 0 and n_p + n_s == n
        npb = n_p // tm
        out_specs = [
            pl.BlockSpec((tm, D_MODEL), lambda i, p: (jnp.minimum(i, npb - 1), 0)),
            pl.BlockSpec((tm, D_MODEL), lambda i, p: (jnp.maximum(i - npb, 0), 0)),
        ]
        out_shape = [jax.ShapeDtypeStruct((n_p, D_MODEL), F32), jax.ShapeDtypeStruct((n_s, D_MODEL), F32)]
    return pl.pallas_call(
        functools.partial(_combine_kernel, tm=tm, n_prompt_blocks=npb),
        grid_spec=pltpu.PrefetchScalarGridSpec(
            num_scalar_prefetch=1,
            grid=(n // tm,),
            in_specs=[
                pl.BlockSpec((tm, D_MODEL), row),
                pl.BlockSpec((tm, ROUTE_LANES), row),
                pl.BlockSpec(memory_space=pl.ANY),
                pl.BlockSpec((1, D_MODEL), const),
                pl.BlockSpec((1, D_MODEL), const),
            ],
            out_specs=out_specs,
            scratch_shapes=[pltpu.VMEM((2, tm * TOKEN_ROWS, LANE), jnp.uint32), pltpu.SemaphoreType.DMA(())],
        ),
        out_shape=out_shape,
        compiler_params=_params(("arbitrary",)),
        name="moe_combine_ln" if split is None else "moe_combine_ln_final",
    )(pos, x1, rw, ys, g, b)


def kernel(x_prompt, x_sample, state_gla, cache_pool, ln_in_g, ln_in_b, w_in, w_forget_up, b_forget, gla_norm_g, w_pool, pool_scale, w_out, ln1_g, ln1_b, router_group_w, router_group_b, router_expert_w, router_expert_b, w_exp_gate, w_exp_up, w_exp_down, ln2_g, ln2_b):
    n_pb, seq, d = x_prompt.shape
    n_sb, dseq, _ = x_sample.shape
    n_p, n_s = n_pb * seq, n_sb * dseq
    depth = w_in.shape[0]
    row2 = lambda v: v.reshape(1, -1)

    w_out_b = w_out.astype(BF16)
    w_in_t = jnp.swapaxes(w_in, 1, 2)
    wfu = jnp.concatenate([w_forget_up, jnp.zeros((depth, LANE - GATE_RANK, GLA_KW), F32)], axis=1).astype(BF16)
    wpool = w_pool.astype(BF16)
    wr = jnp.concatenate(
        [router_group_w, router_expert_w.transpose(0, 2, 1, 3).reshape(depth, d, N_EXPERTS),
         jnp.zeros((depth, d, ROUTE_LANES - N_GROUPS - N_EXPERTS), F32)], axis=2)
    br = jnp.concatenate(
        [router_group_b, router_expert_b.reshape(depth, N_EXPERTS),
         jnp.zeros((depth, ROUTE_LANES - N_GROUPS - N_EXPERTS), F32)], axis=1)
    w1 = w_exp_gate.reshape(depth * N_EXPERTS, d, EXPERT_HIDDEN)
    w3 = w_exp_up.reshape(depth * N_EXPERTS, d, EXPERT_HIDDEN)
    w2 = w_exp_down.reshape(depth * N_EXPERTS, EXPERT_HIDDEN, d)
    hist0 = jnp.concatenate([jnp.zeros((depth, n_sb, 1, POOL_WIDTH), F32), cache_pool], axis=2)

    x, xb = _ln_in(x_prompt.reshape(n_p, d), x_sample.reshape(n_s, d), row2(ln_in_g), row2(ln_in_b))
    states_p, hists_p, states_s, hists_s = [], [], [], []
    for l in range(depth):
        qkvg = _in_proj_qkvg(xb, w_in_t, l)
        pa = _in_proj_pa(xb, w_in_t, l)
        mix_w = (wfu[l], row2(b_forget[l]), row2(gla_norm_g[l]), wpool[l], row2(pool_scale[l]))
        mixed, sp, hp = _mixer(qkvg, pa, *mix_w, row0=0, n_seq=n_pb, seq_len=seq, n_hist=0)
        mixed, ss, hs = _mixer(qkvg, pa, *mix_w, row0=n_p, n_seq=n_sb, seq_len=dseq, n_hist=POOL_HIST,
                               init=(state_gla[l], hist0[l]), mixed_in=mixed)
        x1, x1p, ri, rw = _out_proj(mixed, x, w_out_b, row2(ln1_g[l]), row2(ln1_b[l]), wr[l], row2(br[l]), l)
        pos, plan = _dispatch_plan(ri[:, :2])
        xs = _dispatch(pos, x1p)
        ys = _experts(plan, xs, w1, w3, w2, l)
        if l + 1 < depth:
            x, xb = _combine(pos, x1, rw, ys, row2(ln2_g[l]), row2(ln2_b[l]))
        else:
            y_p, y_s = _combine(pos, x1, rw, ys, row2(ln2_g[l]), row2(ln2_b[l]), split=(n_p, n_s))
        states_p.append(sp)
        hists_p.append(hp[:, 1:])
        states_s.append(ss)
        hists_s.append(hs[:, 1:])
    return (y_p.reshape(n_pb, seq, d), y_s.reshape(n_sb, dseq, d),
            jnp.stack(states_p), jnp.stack(hists_p), jnp.stack(states_s), jnp.stack(hists_s))
```

```python
import functools

import jax
import jax.numpy as jnp
from jax import lax
from jax.experimental import pallas as pl
from jax.experimental.pallas import tpu as pltpu

D_MODEL = 2048
DEPTH = 4
CHUNK = 64

GLA_HEADS = 4
GLA_WIDTH = D_MODEL // 2
GLA_DV = GLA_WIDTH // GLA_HEADS
GLA_DK = GLA_DV // 2
GLA_KW = GLA_HEADS * GLA_DK
GATE_RANK = 16
GATE_TAU = 16.0

POOL_WINDOWS = (2, 4, 8, 16)
POOL_GROUPS = len(POOL_WINDOWS)
POOL_WIDTH = D_MODEL - GLA_WIDTH
POOL_GC = POOL_WIDTH // POOL_GROUPS
POOL_HIST = max(POOL_WINDOWS) - 1
HIST_ROWS = POOL_HIST + 1

Q_OFF = 0
K_OFF = Q_OFF + GLA_KW
V_OFF = K_OFF + GLA_KW
G_OFF = V_OFF + GLA_WIDTH
A_OFF = G_OFF + GLA_WIDTH
P_OFF = A_OFF + GATE_RANK
IN_COLS = P_OFF + POOL_WIDTH
PA_COLS = POOL_WIDTH + 128

N_GROUPS = 4
N_EXP = 8
N_EXPERTS = N_GROUPS * N_EXP
EXPERT_HIDDEN = D_MODEL // 4
ROUTE_LANES = 128
ROUTER_CORRECTION = 0.5

DN_ALPHA = (2 * DEPTH) ** 0.25
LN_EPS = 1e-5
RMS_EPS = 1e-6

VMEM_LIMIT_BYTES = 56 * 1024 * 1024
LANE = 128
TOKEN_ROWS = D_MODEL // 2 // LANE
EXPERT_TILE = 256
BF16 = jnp.bfloat16
F32 = jnp.float32


def _params(sem, vmem=VMEM_LIMIT_BYTES):
    return pltpu.CompilerParams(dimension_semantics=sem, vmem_limit_bytes=vmem)


def _row_tile(n, prefs=(640, 512, 320, 256, 128)):
    for t in prefs:
        if n % t == 0:
            return t
    raise ValueError(f"no row tile for {n} rows")


def _layer_norm(x, g, b):
    mu = jnp.mean(x, axis=-1, keepdims=True)
    xc = x - mu
    var = jnp.mean(xc * xc, axis=-1, keepdims=True)
    return xc * lax.rsqrt(var + LN_EPS) * g + b


def _dot(a, b):
    return jnp.dot(a, b, preferred_element_type=F32)


def _dot_nt(a, b):
    return lax.dot_general(a, b, (((1,), (1,)), ((), ())), preferred_element_type=F32)


def _pack_pairs(x):
    k = x.shape[1] // 2
    hi = lax.bitcast_convert_type(x[:, :k].astype(BF16).astype(F32), jnp.uint32)
    lo = lax.bitcast_convert_type(x[:, k:].astype(BF16).astype(F32), jnp.uint32)
    return hi | (lo >> 16)


def _unpack_pairs(w):
    hi = lax.bitcast_convert_type(w & jnp.uint32(0xFFFF0000), F32)
    lo = lax.bitcast_convert_type(w << 16, F32)
    return jnp.concatenate([hi, lo], axis=1)


def _store_token_tiles(ref, words):
    m = words.shape[0]
    for j in range(TOKEN_ROWS):
        ref[pl.ds(j, m, stride=TOKEN_ROWS), :] = words[:, j * LANE:(j + 1) * LANE]


def _load_token_tiles(ref, m):
    return jnp.concatenate([ref[pl.ds(j, m, stride=TOKEN_ROWS), :] for j in range(TOKEN_ROWS)], axis=1)


def _dot_split(m_bf16, x):
    hi = x.astype(BF16)
    lo = (x - hi.astype(F32)).astype(BF16)
    return _dot(m_bf16, hi) + _dot(m_bf16, lo)


def _ln_in_kernel(xp_ref, xs_ref, g_ref, b_ref, x_ref, xb_ref, *, n_prompt_blocks):
    i = pl.program_id(0)
    x = jnp.where(i < n_prompt_blocks, xp_ref[...], xs_ref[...])
    y = _layer_norm(x, g_ref[...], b_ref[...])
    x_ref[...] = y
    xb_ref[...] = y.astype(BF16)


def _ln_in(xp, xs, g, b):
    n_p, n_s = xp.shape[0], xs.shape[0]
    tm = n_s
    assert n_p % tm == 0
    npb = n_p // tm
    n = n_p + n_s
    return pl.pallas_call(
        functools.partial(_ln_in_kernel, n_prompt_blocks=npb),
        grid=(npb + 1,),
        in_specs=[
            pl.BlockSpec((tm, D_MODEL), lambda i: (jnp.minimum(i, npb - 1), 0)),
            pl.BlockSpec((tm, D_MODEL), lambda i: (0, 0)),
            pl.BlockSpec((1, D_MODEL), lambda i: (0, 0)),
            pl.BlockSpec((1, D_MODEL), lambda i: (0, 0)),
        ],
        out_specs=[
            pl.BlockSpec((tm, D_MODEL), lambda i: (i, 0)),
            pl.BlockSpec((tm, D_MODEL), lambda i: (i, 0)),
        ],
        out_shape=[jax.ShapeDtypeStruct((n, D_MODEL), F32), jax.ShapeDtypeStruct((n, D_MODEL), BF16)],
        compiler_params=_params(("arbitrary",)),
        name="ln_in",
    )(xp, xs, g, b)


def _proj_qkvg_kernel(x_ref, wt_ref, o_ref, wb_ref):
    @pl.when(pl.program_id(1) == 0)
    def _():
        wb_ref[...] = wt_ref[...].astype(BF16)

    o_ref[...] = _dot_nt(x_ref[...], wb_ref[...]).astype(o_ref.dtype)


def _in_proj_qkvg(xb, w_in_t, layer):
    n = xb.shape[0]
    tm, tn = _row_tile(n), 1024
    return pl.pallas_call(
        _proj_qkvg_kernel,
        grid=(A_OFF // tn, n // tm),
        in_specs=[
            pl.BlockSpec((tm, D_MODEL), lambda j, m: (m, 0)),
            pl.BlockSpec((None, tn, D_MODEL), lambda j, m: (layer, j, 0)),
        ],
        out_specs=pl.BlockSpec((tm, tn), lambda j, m: (m, j)),
        out_shape=jax.ShapeDtypeStruct((n, A_OFF), BF16),
        scratch_shapes=[pltpu.VMEM((tn, D_MODEL), BF16)],
        compiler_params=_params(("arbitrary", "arbitrary")),
        name="in_proj_qkvg",
    )(xb, w_in_t)


def _proj_pa_kernel(x_ref, wt_hbm, o_ref, stage_ref, wb_ref, sem, *, layer):
    n_tail = IN_COLS - A_OFF

    @pl.when(pl.program_id(0) == 0)
    def _():
        copies = (
            pltpu.make_async_copy(wt_hbm.at[layer, pl.ds(P_OFF, POOL_WIDTH)], stage_ref.at[pl.ds(0, POOL_WIDTH)],
                                  sem.at[0]),
            pltpu.make_async_copy(wt_hbm.at[layer, pl.ds(A_OFF, GATE_RANK)],
                                  stage_ref.at[pl.ds(POOL_WIDTH, GATE_RANK)], sem.at[1]),
        )
        for c in copies:
            c.start()
        for c in copies:
            c.wait()
        wb_ref[0:n_tail, :] = stage_ref[...].astype(BF16)
        wb_ref[n_tail:, :] = jnp.zeros((PA_COLS - n_tail, D_MODEL), BF16)

    o_ref[...] = _dot_nt(x_ref[...], wb_ref[...])


def _in_proj_pa(xb, w_in_t, layer):
    n = xb.shape[0]
    tm = _row_tile(n)
    return pl.pallas_call(
        functools.partial(_proj_pa_kernel, layer=layer),
        grid=(n // tm,),
        in_specs=[
            pl.BlockSpec((tm, D_MODEL), lambda m: (m, 0)),
            pl.BlockSpec(memory_space=pl.ANY),
        ],
        out_specs=pl.BlockSpec((tm, PA_COLS), lambda m: (m, 0)),
        out_shape=jax.ShapeDtypeStruct((n, PA_COLS), F32),
        scratch_shapes=[
            pltpu.VMEM((IN_COLS - A_OFF, D_MODEL), F32),
            pltpu.VMEM((PA_COLS, D_MODEL), BF16),
            pltpu.SemaphoreType.DMA((2,)),
        ],
        compiler_params=_params(("arbitrary",)),
        name="in_proj_pa",
    )(xb, w_in_t)


def _mixer_kernel(*refs, n_seq, n_blk, n_chunks, valid, n_hist, has_init, fill_tail):
    if has_init:
        (qkvg_ref, pa_ref, wfu_ref, bf_ref, gain_ref, wpool_ref, pscale_ref, s0_ref, h0_ref, _mixed_in,
         mixed_ref, sout_ref, hout_ref, s_scr, z_scr) = refs
    else:
        (qkvg_ref, pa_ref, wfu_ref, bf_ref, gain_ref, wpool_ref, pscale_ref,
         mixed_ref, sout_ref, hout_ref, s_scr, z_scr) = refs
    C = CHUNK
    step = pl.program_id(0)
    blk = step % n_blk

    def pad_rows(x):
        if valid == C:
            return x
        return jnp.concatenate([x, jnp.zeros((C - valid, x.shape[1]), x.dtype)], axis=0)

    def process():
        @pl.when(blk == 0)
        def _():
            if has_init:
                s_scr[...] = s0_ref[...]
                z_scr[0:C - HIST_ROWS, :] = jnp.zeros((C - HIST_ROWS, POOL_WIDTH), F32)
                z_scr[C - HIST_ROWS:C, :] = h0_ref[...]
            else:
                s_scr[...] = jnp.zeros(s_scr.shape, F32)
                z_scr[...] = jnp.zeros(z_scr.shape, F32)

        row = lax.broadcasted_iota(jnp.int32, (C, C), 0)
        col = lax.broadcasted_iota(jnp.int32, (C, C), 1)
        causal = row >= col
        tri = causal.astype(BF16)
        t_idx = lax.broadcasted_iota(jnp.int32, (C, 2 * C), 0)
        s_idx = lax.broadcasted_iota(jnp.int32, (C, 2 * C), 1) - C
        lag = t_idx - s_idx
        bands = [((lag >= 0) & (lag < w)).astype(BF16) for w in POOL_WINDOWS]

        def chunk(j, carry):
            if n_chunks == 1:
                rows = slice(0, valid)
            else:
                rows = pl.ds(pl.multiple_of(j * C, C), C)
            qkvg = pad_rows(qkvg_ref[rows, :])
            pa = pad_rows(pa_ref[rows, :])
            q = qkvg[:, Q_OFF:K_OFF].astype(F32)
            k = qkvg[:, K_OFF:V_OFF].astype(F32)
            v = qkvg[:, V_OFF:G_OFF]
            g = qkvg[:, G_OFF:A_OFF].astype(F32)
            zc = pa[:, :POOL_WIDTH]
            a = pa[:, POOL_WIDTH:]

            la = jax.nn.log_sigmoid(_dot(a.astype(BF16), wfu_ref[...]) + bf_ref[...]) * (1.0 / GATE_TAU)
            if valid != C:
                la = jnp.where(lax.broadcasted_iota(jnp.int32, la.shape, 0) < valid, la, 0.0)
            bcum = _dot_split(tri, la)
            b_last = bcum[C - 1:C, :]
            q_t = (q * jnp.exp(bcum) * (GLA_DK ** -0.5)).astype(BF16)
            k_t = (k * jnp.exp(-bcum)).astype(BF16)
            k_dec = k * jnp.exp(b_last - bcum)
            e_last = jnp.exp(b_last)
            outs = []
            for h in range(GLA_HEADS):
                ks = slice(h * GLA_DK, (h + 1) * GLA_DK)
                vs = slice(h * GLA_DV, (h + 1) * GLA_DV)
                s_h = s_scr[h]
                att = jnp.where(causal, _dot_nt(q_t[:, ks], k_t[:, ks]), 0.0)
                o = _dot(att.astype(BF16), v[:, vs]) + _dot(q_t[:, ks], s_h.astype(BF16))
                k_aug = jnp.concatenate([k_dec[:, ks], jnp.zeros((GLA_DK - C, GLA_DK), F32)], axis=0)
                v_aug = jnp.concatenate([v[:, vs], jnp.zeros((GLA_DK - C, GLA_DV), BF16)], axis=0)
                decay = jnp.broadcast_to(e_last[:, ks], (GLA_DK, GLA_DK)).T
                decay = jnp.concatenate([decay] * (GLA_DV // GLA_DK), axis=1)
                s_scr[h] = decay * s_h + _dot(k_aug.T.astype(BF16), v_aug)
                outs.append(o * lax.rsqrt(jnp.mean(o * o, axis=-1, keepdims=True) + RMS_EPS))
            gla = jnp.concatenate(outs, axis=1) * gain_ref[...] * (g * jax.nn.sigmoid(g))

            z_ext = jnp.concatenate([z_scr[...], zc], axis=0)
            t0 = (blk * n_chunks + j) * C + n_hist + 1
            avail = lax.broadcasted_iota(jnp.int32, (C, POOL_GC), 0) + t0
            pooled = []
            for gi, w in enumerate(POOL_WINDOWS):
                cs = slice(gi * POOL_GC, (gi + 1) * POOL_GC)
                cnt = jnp.minimum(avail, w).astype(F32)
                m = _dot_split(bands[gi], z_ext[:, cs]) / cnt - zc[:, cs]
                pooled.append(_dot(m.astype(BF16), wpool_ref[gi]))
            pool = jnp.concatenate(pooled, axis=1) * pscale_ref[...]
            z_scr[...] = zc

            res = jnp.concatenate([gla, pool], axis=1).astype(BF16)
            mixed_ref[rows, :] = res[0:valid]
            return carry

        if n_chunks == 1:
            chunk(0, 0)
        else:
            lax.fori_loop(0, n_chunks, chunk, 0, unroll=2)

        @pl.when(blk == n_blk - 1)
        def _():
            sout_ref[...] = s_scr[...]
            hout_ref[...] = z_scr[valid - HIST_ROWS:valid, :]

    if fill_tail:
        pl.when(step < n_seq * n_blk)(process)

        @pl.when(step == n_seq * n_blk)
        def _():
            mixed_ref[...] = jnp.zeros(mixed_ref.shape, BF16)
    else:
        process()


def _mixer_block_kernel(*refs, n_seq, n_blk, rows_valid, n_hist, has_init, fill_tail):
    if has_init:
        (qkvg_ref, pa_ref, wfu_ref, bf_ref, gain_ref, wpool_ref, pscale_ref, s0_ref, h0_ref, _mixed_in,
         mixed_ref, sout_ref, hout_ref, s_scr, z_scr) = refs
    else:
        (qkvg_ref, pa_ref, wfu_ref, bf_ref, gain_ref, wpool_ref, pscale_ref,
         mixed_ref, sout_ref, hout_ref, s_scr, z_scr) = refs
    C = CHUNK
    R = max(rows_valid, 2 * C)
    nc = R // C
    step = pl.program_id(0)
    blk = step % n_blk

    def pad_rows(x):
        if rows_valid == R:
            return x
        return jnp.concatenate([x, jnp.zeros((R - rows_valid, x.shape[1]), x.dtype)], axis=0)

    def process():
        @pl.when(blk == 0)
        def _():
            if has_init:
                s_scr[...] = s0_ref[...]
                z_scr[0:C - HIST_ROWS, :] = jnp.zeros((C - HIST_ROWS, POOL_WIDTH), F32)
                z_scr[C - HIST_ROWS:C, :] = h0_ref[...]
            else:
                s_scr[...] = jnp.zeros(s_scr.shape, F32)
                z_scr[...] = jnp.zeros(z_scr.shape, F32)

        qkvg = pad_rows(qkvg_ref[...])
        pa = pad_rows(pa_ref[...])
        q = qkvg[:, Q_OFF:K_OFF].astype(F32)
        k = qkvg[:, K_OFF:V_OFF].astype(F32)
        v = qkvg[:, V_OFF:G_OFF]
        g = qkvg[:, G_OFF:A_OFF].astype(F32)
        zc = pa[:, :POOL_WIDTH]
        a = pa[:, POOL_WIDTH:]

        row = lax.broadcasted_iota(jnp.int32, (R, R), 0)
        col = lax.broadcasted_iota(jnp.int32, (R, R), 1)
        causal = (row // C == col // C) & (row >= col)
        la = jax.nn.log_sigmoid(_dot(a.astype(BF16), wfu_ref[...]) + bf_ref[...]) * (1.0 / GATE_TAU)
        if rows_valid != R:
            la = jnp.where(lax.broadcasted_iota(jnp.int32, la.shape, 0) < rows_valid, la, 0.0)
        bcum = _dot_split(causal.astype(BF16), la)
        tot = bcum.reshape(nc, C, GLA_KW)[:, C - 1:C, :]
        tot = jnp.broadcast_to(tot, (nc, C, GLA_KW)).reshape(R, GLA_KW)
        q_t = (q * jnp.exp(bcum) * (GLA_DK ** -0.5)).astype(BF16)
        k_t = (k * jnp.exp(-bcum)).astype(BF16)
        k_dec = k * jnp.exp(tot - bcum)
        e_tot = jnp.exp(tot)
        first_half = lax.broadcasted_iota(jnp.int32, (GLA_DK, 2 * C), 1) < C
        outs = []
        for h in range(GLA_HEADS):
            ks = slice(h * GLA_DK, (h + 1) * GLA_DK)
            vs = slice(h * GLA_DV, (h + 1) * GLA_DV)
            att = jnp.where(causal, _dot_nt(q_t[:, ks], k_t[:, ks]), 0.0)
            o_intra = _dot(att.astype(BF16), v[:, vs])
            k_dec_t = k_dec[:, ks].T
            e_tot_t = e_tot[:, ks].T
            updates = []
            for m in range(nc // 2):
                kt_pair = k_dec_t[:, 2 * C * m:2 * C * (m + 1)]
                v_pair = v[2 * C * m:2 * C * (m + 1), vs]
                updates.append(_dot(jnp.where(first_half, kt_pair, 0.0).astype(BF16), v_pair))
                updates.append(_dot(jnp.where(first_half, 0.0, kt_pair).astype(BF16), v_pair))
            s_h = s_scr[h]
            o_inter = []
            for j in range(nc):
                o_inter.append(_dot(q_t[C * j:C * (j + 1), ks], s_h.astype(BF16)))
                decay = jnp.broadcast_to(e_tot_t[:, C * j:C * j + 1], (GLA_DK, GLA_DV))
                s_h = decay * s_h + updates[j]
            s_scr[h] = s_h
            o = o_intra + jnp.concatenate(o_inter, axis=0)
            outs.append(o * lax.rsqrt(jnp.mean(o * o, axis=-1, keepdims=True) + RMS_EPS))
        gla = jnp.concatenate(outs, axis=1) * gain_ref[...] * (g * jax.nn.sigmoid(g))

        z_all = jnp.concatenate([z_scr[...], zc], axis=0)
        z_hi = z_all.astype(BF16)
        z_lo = (z_all - z_hi.astype(F32)).astype(BF16)
        t_idx = lax.broadcasted_iota(jnp.int32, (C, 2 * C), 0)
        s_idx = lax.broadcasted_iota(jnp.int32, (C, 2 * C), 1) - C
        lag = t_idx - s_idx
        avail0 = lax.broadcasted_iota(jnp.int32, (C, POOL_GC), 0) + (blk * rows_valid + n_hist + 1)
        pooled = []
        for gi, w in enumerate(POOL_WINDOWS):
            cs = slice(gi * POOL_GC, (gi + 1) * POOL_GC)
            band = ((lag >= 0) & (lag < w)).astype(BF16)
            means = []
            for j in range(nc):
                win = slice(C * j, C * (j + 2))
                sums = _dot(band, z_hi[win, cs]) + _dot(band, z_lo[win, cs])
                cnt = jnp.minimum(avail0 + C * j, w).astype(F32)
                means.append(sums / cnt - zc[C * j:C * (j + 1), cs])
            pooled.append(_dot(jnp.concatenate(means, axis=0).astype(BF16), wpool_ref[gi]))
        pool = jnp.concatenate(pooled, axis=1) * pscale_ref[...]
        z_scr[...] = zc[R - C:R, :]

        res = jnp.concatenate([gla, pool], axis=1).astype(BF16)
        mixed_ref[...] = res[0:rows_valid]

        @pl.when(blk == n_blk - 1)
        def _():
            sout_ref[...] = s_scr[...]
            hout_ref[...] = zc[rows_valid - HIST_ROWS:rows_valid, :]

    if fill_tail:
        pl.when(step < n_seq * n_blk)(process)

        @pl.when(step == n_seq * n_blk)
        def _():
            mixed_ref[...] = jnp.zeros(mixed_ref.shape, BF16)
    else:
        process()


def _mixer(qkvg, pa, wfu, bfg, gain, wpool, pscale, *, row0, n_seq, seq_len, n_hist, init=None, mixed_in=None):
    n = qkvg.shape[0]
    if seq_len % (2 * CHUNK) == 0:
        rb = _row_tile(seq_len, (512, 256, 128))
    else:
        rb = seq_len
        assert HIST_ROWS <= seq_len < CHUNK and init is not None
    assert seq_len % rb == 0 and row0 % rb == 0
    n_blk = seq_len // rb
    base = row0 // rb
    n_steps = n_seq * n_blk
    has_init = init is not None
    assert has_init == (mixed_in is not None)
    tail_rows = n - (row0 + n_seq * seq_len)
    fill_tail = (not has_init) and tail_rows > 0
    assert tail_rows <= rb

    rows = lambda t: (base + t, 0)
    const2 = lambda t: (0, 0)
    seq3 = lambda t: (jnp.minimum(t // n_blk, n_seq - 1), 0, 0)
    seq4 = lambda t: (jnp.minimum(t // n_blk, n_seq - 1), 0, 0, 0)

    in_specs = [
        pl.BlockSpec((rb, A_OFF), rows),
        pl.BlockSpec((rb, PA_COLS), rows),
        pl.BlockSpec((LANE, GLA_KW), const2),
        pl.BlockSpec((1, GLA_KW), const2),
        pl.BlockSpec((1, GLA_WIDTH), const2),
        pl.BlockSpec((POOL_GROUPS, POOL_GC, POOL_GC), lambda t: (0, 0, 0)),
        pl.BlockSpec((1, POOL_WIDTH), const2),
    ]
    args = [qkvg, pa, wfu, bfg, gain, wpool, pscale]
    aliases = {}
    if has_init:
        in_specs += [
            pl.BlockSpec((None, GLA_HEADS, GLA_DK, GLA_DV), seq4),
            pl.BlockSpec((None, HIST_ROWS, POOL_WIDTH), seq3),
            pl.BlockSpec(memory_space=pl.ANY),
        ]
        args += [init[0], init[1], mixed_in]
        aliases = {len(args) - 1: 0}
    return pl.pallas_call(
        functools.partial(_mixer_block_kernel, n_seq=n_seq, n_blk=n_blk, rows_valid=rb, n_hist=n_hist,
                          has_init=has_init, fill_tail=fill_tail),
        grid=(n_steps + int(fill_tail),),
        in_specs=in_specs,
        out_specs=[
            pl.BlockSpec((rb, D_MODEL), rows),
            pl.BlockSpec((None, GLA_HEADS, GLA_DK, GLA_DV), seq4),
            pl.BlockSpec((None, HIST_ROWS, POOL_WIDTH), seq3),
        ],
        out_shape=[
            jax.ShapeDtypeStruct((n, D_MODEL), BF16),
            jax.ShapeDtypeStruct((n_seq, GLA_HEADS, GLA_DK, GLA_DV), F32),
            jax.ShapeDtypeStruct((n_seq, HIST_ROWS, POOL_WIDTH), F32),
        ],
        scratch_shapes=[
            pltpu.VMEM((GLA_HEADS, GLA_DK, GLA_DV), F32),
            pltpu.VMEM((CHUNK, POOL_WIDTH), F32),
        ],
        input_output_aliases=aliases,
        compiler_params=_params(("arbitrary",)),
        name="mixer_init" if has_init else "mixer",
    )(*args)


def _route(logits):
    lane = lax.broadcasted_iota(jnp.int32, logits.shape, 1).astype(F32)
    neg = -jnp.inf
    first_lane = lambda hit: jnp.min(jnp.where(hit, lane, float(ROUTE_LANES)), axis=-1, keepdims=True)
    lg = jnp.where(lane < N_GROUPS, logits, neg)
    un = jnp.exp(lg - jnp.max(lg, axis=-1, keepdims=True))
    pg = un / jnp.sum(un, axis=-1, keepdims=True)
    top_pg = jnp.max(pg, axis=-1, keepdims=True)
    gsel = first_lane((pg == top_pg) & (lane < N_GROUPS))
    first = N_GROUPS + gsel * N_EXP
    le = jnp.where((lane >= first) & (lane < first + N_EXP), logits, neg)
    v0 = jnp.max(le, axis=-1, keepdims=True)
    i0 = first_lane(le == v0)
    le = jnp.where(lane == i0, neg, le)
    v1 = jnp.max(le, axis=-1, keepdims=True)
    i1 = first_lane(le == v1)
    u1 = jnp.exp(v1 - v0)
    den = 1.0 + u1
    e0 = (i0 - N_GROUPS).astype(jnp.int32)
    e1 = (i1 - N_GROUPS).astype(jnp.int32)
    return e0, e1, (1.0 / den) * top_pg, (u1 / den) * top_pg


def _out_proj_kernel(mixed_ref, x_ref, w_ref, g_ref, b_ref, wr_ref, br_ref, x1_ref, x1p_ref, ri_ref, rw_ref,
                     wrs_ref):
    tm = x_ref.shape[0]

    @pl.when(pl.program_id(0) == 0)
    def _():
        wr = wr_ref[...]
        wr_hi = wr.astype(BF16)
        wrs_ref[:, :ROUTE_LANES] = wr_hi
        wrs_ref[:, ROUTE_LANES:] = (wr - wr_hi.astype(F32)).astype(BF16)

    y = _dot(mixed_ref[...], w_ref[...])
    x1 = _layer_norm(DN_ALPHA * x_ref[...] + y, g_ref[...], b_ref[...])
    x1_ref[...] = x1
    _store_token_tiles(x1p_ref, _pack_pairs(x1))
    x_hi = x1.astype(BF16)
    x_lo = (x1 - x_hi.astype(F32)).astype(BF16)
    prod = _dot(jnp.concatenate([x_hi, x_lo], axis=0), wrs_ref[...])
    corr = prod[tm:, :ROUTE_LANES] + prod[:tm, ROUTE_LANES:]
    logits = prod[:tm, :ROUTE_LANES] + ROUTER_CORRECTION * corr + br_ref[...]
    e0, e1, w0, w1 = _route(logits)
    lane = lax.broadcasted_iota(jnp.int32, logits.shape, 1)
    ri_ref[...] = jnp.where(lane == 0, e0, jnp.where(lane == 1, e1, 0))
    rw_ref[...] = jnp.where(lane == 0, w0, jnp.where(lane == 1, w1, 0.0))


def _out_proj(mixed, x, w_out, g, b, wr, br, layer):
    n = x.shape[0]
    tm = _row_tile(n, (640, 416, 320, 256, 128))
    row = lambda i: (i, 0)
    const = lambda i: (0, 0)
    return pl.pallas_call(
        _out_proj_kernel,
        grid=(n // tm,),
        in_specs=[
            pl.BlockSpec((tm, D_MODEL), row),
            pl.BlockSpec((tm, D_MODEL), row),
            pl.BlockSpec((None, D_MODEL, D_MODEL), lambda i: (layer, 0, 0), pipeline_mode=pl.Buffered(1)),
            pl.BlockSpec((1, D_MODEL), const),
            pl.BlockSpec((1, D_MODEL), const),
            pl.BlockSpec((D_MODEL, ROUTE_LANES), const),
            pl.BlockSpec((1, ROUTE_LANES), const),
        ],
        out_specs=[
            pl.BlockSpec((tm, D_MODEL), row),
            pl.BlockSpec((tm * TOKEN_ROWS, LANE), row),
            pl.BlockSpec((tm, ROUTE_LANES), row),
            pl.BlockSpec((tm, ROUTE_LANES), row),
        ],
        out_shape=[
            jax.ShapeDtypeStruct((n, D_MODEL), F32),
            jax.ShapeDtypeStruct((n * TOKEN_ROWS, LANE), jnp.uint32),
            jax.ShapeDtypeStruct((n, ROUTE_LANES), jnp.int32),
            jax.ShapeDtypeStruct((n, ROUTE_LANES), F32),
        ],
        scratch_shapes=[pltpu.VMEM((D_MODEL, 2 * ROUTE_LANES), BF16)],
        compiler_params=_params(("arbitrary",)),
        name="out_proj_ln_route",
    )(mixed, x, w_out, g, b, wr, br)


def _sort_kernel(ri_ref, pos_ref, cnt_ref, *, tb):
    n = ri_ref.shape[0]
    n_blocks = n // tb
    lane = lax.broadcasted_iota(jnp.int32, (tb, ROUTE_LANES), 1)
    ones = jnp.ones((8, tb), BF16)

    def onehots(b):
        ri = ri_ref[pl.ds(pl.multiple_of(b * tb, tb), tb), :]
        return lane == ri[:, 0:1], lane == ri[:, 1:2]

    def count(b, acc):
        h0, h1 = onehots(b)
        return acc + _dot(ones, jnp.where(h0 | h1, 1.0, 0.0).astype(BF16))

    counts = lax.fori_loop(0, n_blocks, count, jnp.zeros((8, ROUTE_LANES), F32))
    cnt_ref[...] = counts.astype(jnp.int32)
    before = (lax.broadcasted_iota(jnp.int32, (ROUTE_LANES, ROUTE_LANES), 0)
              < lax.broadcasted_iota(jnp.int32, (ROUTE_LANES, ROUTE_LANES), 1)).astype(BF16)
    c_hi = jnp.floor(counts * (1.0 / ROUTE_LANES))
    c_lo = counts - c_hi * ROUTE_LANES
    offs = (_dot(c_hi.astype(BF16), before) * ROUTE_LANES + _dot(c_lo.astype(BF16), before))[0:1, :]
    earlier = (lax.broadcasted_iota(jnp.int32, (tb, tb), 0) > lax.broadcasted_iota(jnp.int32, (tb, tb), 1)).astype(BF16)

    def place(b, carry):
        h0, h1 = onehots(b)
        both = jnp.where(h0 | h1, 1.0, 0.0).astype(BF16)
        start = _dot(earlier, both) + carry + offs
        p0 = jnp.sum(jnp.where(h0, start, 0.0), axis=-1, keepdims=True)
        p1 = jnp.sum(jnp.where(h1, start, 0.0), axis=-1, keepdims=True)
        out = jnp.where(lane == 0, p0, jnp.where(lane == 1, p1, 0.0))
        pos_ref[pl.ds(pl.multiple_of(b * tb, tb), tb), :] = out.astype(jnp.int32)
        return carry + _dot(ones, both)[0:1, :]

    lax.fori_loop(0, n_blocks, place, jnp.zeros((1, ROUTE_LANES), F32))


def _sort_pairs(ri):
    n = ri.shape[0]
    tb = _row_tile(n, (640, 512, 384, 256, 128))
    pos, cnt = pl.pallas_call(
        functools.partial(_sort_kernel, tb=tb),
        out_shape=[jax.ShapeDtypeStruct((n, ROUTE_LANES), jnp.int32), jax.ShapeDtypeStruct((8, ROUTE_LANES), jnp.int32)],
        compiler_params=pltpu.CompilerParams(vmem_limit_bytes=VMEM_LIMIT_BYTES),
        name="moe_sort",
    )(ri)
    return pos, cnt[0, :N_EXPERTS]


def _dispatch_plan(ri):
    n_pairs = ri.shape[0] * 2
    n_tiles = n_pairs // EXPERT_TILE
    pos, counts = _sort_pairs(ri)
    pos = pos[:, :2].reshape(-1)
    offs = jnp.concatenate([jnp.zeros((1,), jnp.int32), jnp.cumsum(counts)])
    bounds = jnp.sort(jnp.concatenate([jnp.arange(n_tiles, dtype=jnp.int32) * EXPERT_TILE, offs[:N_EXPERTS]]))
    seg_lo = bounds
    seg_hi = jnp.concatenate([bounds[1:], jnp.full((1,), n_pairs, jnp.int32)])
    tile = jnp.minimum(seg_lo // EXPERT_TILE, n_tiles - 1)
    expert = jnp.minimum(jnp.searchsorted(offs[1:], seg_lo, side="right"), N_EXPERTS - 1).astype(jnp.int32)
    lo = seg_lo - tile * EXPERT_TILE
    hi = jnp.where(seg_hi > seg_lo, seg_hi - tile * EXPERT_TILE, lo)
    valid = seg_hi > seg_lo
    e_seen = lax.cummax(jnp.where(valid, expert, -1))
    prev = jnp.concatenate([jnp.full((1,), -1, jnp.int32), e_seen[:-1]])
    fetch = valid & (expert > prev)
    run = jnp.maximum(jnp.cumsum(fetch.astype(jnp.int32)) - 1, 0)
    slot = run % 2
    fetch = fetch.astype(jnp.int32) * jnp.where(run == 0, 2, 1)
    ids = jnp.arange(N_EXPERTS, dtype=jnp.int32)
    later = lax.cummin(jnp.where(counts > 0, ids, N_EXPERTS), reverse=True)
    nxt_of = jnp.concatenate([later[1:], jnp.full((1,), N_EXPERTS, jnp.int32)])
    nxt = jnp.where(nxt_of[expert] < N_EXPERTS, nxt_of[expert], -1)
    sched = (fetch, slot.astype(jnp.int32), nxt.astype(jnp.int32))
    return pos, (tile.astype(jnp.int32), expert, lo.astype(jnp.int32), hi.astype(jnp.int32)) + sched


def _dispatch_kernel(pos_ref, x_ref, xs_hbm, sem, *, tb):
    base = pl.program_id(0) * tb

    def scatter(t, k):
        dst = pl.multiple_of(pos_ref[2 * (base + t) + k] * TOKEN_ROWS, TOKEN_ROWS)
        src = pl.multiple_of(t * TOKEN_ROWS, TOKEN_ROWS)
        return pltpu.make_async_copy(x_ref.at[pl.ds(src, TOKEN_ROWS)], xs_hbm.at[pl.ds(dst, TOKEN_ROWS)], sem)

    def issue(t, c):
        scatter(t, 0).start(priority=0)
        scatter(t, 1).start(priority=1)
        return c

    lax.fori_loop(0, tb, issue, 0, unroll=8)

    def drain(t, c):
        scatter(t, 0).wait()
        scatter(t, 1).wait()
        return c

    lax.fori_loop(0, tb, drain, 0, unroll=8)


def _dispatch(pos, x1):
    n = x1.shape[0] // TOKEN_ROWS
    tb = _row_tile(n)
    return pl.pallas_call(
        functools.partial(_dispatch_kernel, tb=tb),
        grid_spec=pltpu.PrefetchScalarGridSpec(
            num_scalar_prefetch=1,
            grid=(n // tb,),
            in_specs=[pl.BlockSpec((tb * TOKEN_ROWS, LANE), lambda i, p: (i, 0))],
            out_specs=pl.BlockSpec(memory_space=pl.ANY),
            scratch_shapes=[pltpu.SemaphoreType.DMA(())],
        ),
        out_shape=jax.ShapeDtypeStruct((2 * n * TOKEN_ROWS, LANE), x1.dtype),
        compiler_params=_params(("arbitrary",)),
        name="moe_dispatch",
    )(pos, x1)


def _experts_kernel(tile_ref, exp_ref, lo_ref, hi_ref, fetch_ref, slot_ref, nxt_ref,
                    xs_ref, w1_hbm, w3_hbm, w2_hbm, o_ref, w1_buf, w3_buf, w2_buf, sem, *, layer):
    i = pl.program_id(0)
    lo, hi = lo_ref[i], hi_ref[i]
    slot = slot_ref[i]

    def weight_copies(expert, s):
        e = layer * N_EXPERTS + expert
        return (pltpu.make_async_copy(w1_hbm.at[e], w1_buf.at[s], sem.at[s, 0]),
                pltpu.make_async_copy(w3_hbm.at[e], w3_buf.at[s], sem.at[s, 1]),
                pltpu.make_async_copy(w2_hbm.at[e], w2_buf.at[s], sem.at[s, 2]))

    @pl.when(fetch_ref[i] == 2)
    def _():
        for c in weight_copies(exp_ref[i], slot):
            c.start()

    @pl.when(fetch_ref[i] > 0)
    def _():
        for c in weight_copies(exp_ref[i], slot):
            c.wait()

        @pl.when(nxt_ref[i] >= 0)
        def _():
            for c in weight_copies(nxt_ref[i], 1 - slot):
                c.start()

    def compute():
        x = _unpack_pairs(_load_token_tiles(xs_ref, EXPERT_TILE)).astype(BF16)
        h1 = _dot(x, w1_buf[slot].astype(BF16))
        h3 = _dot(x, w3_buf[slot].astype(BF16))
        a = (h1 * jax.nn.sigmoid(h1) * h3).astype(BF16)
        return _pack_pairs(_dot(a, w2_buf[slot].astype(BF16)))

    @pl.when((hi > lo) & (lo == 0))
    def _():
        _store_token_tiles(o_ref, compute())

    @pl.when((hi > lo) & (lo > 0))
    def _():
        r = lax.broadcasted_iota(jnp.int32, (EXPERT_TILE, D_MODEL // 2), 0)
        old = _load_token_tiles(o_ref, EXPERT_TILE)
        _store_token_tiles(o_ref, jnp.where((r >= lo) & (r < hi), compute(), old))


def _experts(plan, xs, w1, w3, w2, layer):
    n_items = plan[0].shape[0]
    xrow = lambda i, t, *_: (t[i], 0)
    any_spec = pl.BlockSpec(memory_space=pl.ANY)
    return pl.pallas_call(
        functools.partial(_experts_kernel, layer=layer),
        grid_spec=pltpu.PrefetchScalarGridSpec(
            num_scalar_prefetch=len(plan),
            grid=(n_items,),
            in_specs=[pl.BlockSpec((EXPERT_TILE * TOKEN_ROWS, LANE), xrow), any_spec, any_spec, any_spec],
            out_specs=pl.BlockSpec((EXPERT_TILE * TOKEN_ROWS, LANE), xrow),
            scratch_shapes=[
                pltpu.VMEM((2, D_MODEL, EXPERT_HIDDEN), F32),
                pltpu.VMEM((2, D_MODEL, EXPERT_HIDDEN), F32),
                pltpu.VMEM((2, EXPERT_HIDDEN, D_MODEL), F32),
                pltpu.SemaphoreType.DMA((2, 3)),
            ],
        ),
        out_shape=jax.ShapeDtypeStruct(xs.shape, xs.dtype),
        compiler_params=_params(("arbitrary",)),
        name="moe_experts",
    )(*plan, xs, w1, w3, w2)


def _combine_kernel(pos_ref, x1_ref, rw_ref, ys_hbm, g_ref, b_ref, *rest, tm, n_prompt_blocks):
    if n_prompt_blocks is None:
        x2_ref, x2b_ref, buf, sem = rest
    else:
        yp_ref, ysm_ref, buf, sem = rest
    i = pl.program_id(0)
    slot = i % 2

    def gather(step, s, t, k):
        src = pl.multiple_of(pos_ref[2 * (step * tm + t) + k] * TOKEN_ROWS, TOKEN_ROWS)
        dst = pl.multiple_of(t * TOKEN_ROWS, TOKEN_ROWS)
        return pltpu.make_async_copy(ys_hbm.at[pl.ds(src, TOKEN_ROWS)], buf.at[s, k, pl.ds(dst, TOKEN_ROWS)],
                                     sem.at[s])

    def issue(step, s):
        def body(t, c):
            gather(step, s, t, 0).start(priority=0)
            gather(step, s, t, 1).start(priority=1)
            return c

        lax.fori_loop(0, tm, body, 0, unroll=8)

    @pl.when(i == 0)
    def _():
        issue(0, 0)

    @pl.when(i + 1 < pl.num_programs(0))
    def _():
        issue(i + 1, 1 - slot)

    def drain(t, c):
        gather(i, slot, t, 0).wait()
        gather(i, slot, t, 1).wait()
        return c

    lax.fori_loop(0, tm, drain, 0, unroll=8)

    rw = rw_ref[...]
    f = (rw[:, 0:1] * _unpack_pairs(_load_token_tiles(buf.at[slot, 0], tm))
         + rw[:, 1:2] * _unpack_pairs(_load_token_tiles(buf.at[slot, 1], tm)))
    x2 = _layer_norm(DN_ALPHA * x1_ref[...] + f, g_ref[...], b_ref[...])
    if n_prompt_blocks is None:
        x2_ref[...] = x2
        x2b_ref[...] = x2.astype(BF16)
    else:
        @pl.when(i < n_prompt_blocks)
        def _():
            yp_ref[...] = x2

        @pl.when(i >= n_prompt_blocks)
        def _():
            ysm_ref[...] = x2


def _combine(pos, x1, rw, ys, g, b, *, split=None):
    n = x1.shape[0]
    row = lambda i, p: (i, 0)
    const = lambda i, p: (0, 0)
    if split is None:
        tm, npb = _row_tile(n, (320, 256, 128)), None
        out_specs = [pl.BlockSpec((tm, D_MODEL), row), pl.BlockSpec((tm, D_MODEL), row)]
        out_shape = [jax.ShapeDtypeStruct((n, D_MODEL), F32), jax.ShapeDtypeStruct((n, D_MODEL), BF16)]
    else:
        n_p, n_s = split
        tm = n_s
        assert n_p % tm == 0 and n_p + n_s == n
        npb = n_p // tm
        out_specs = [
            pl.BlockSpec((tm, D_MODEL), lambda i, p: (jnp.minimum(i, npb - 1), 0)),
            pl.BlockSpec((tm, D_MODEL), lambda i, p: (jnp.maximum(i - npb, 0), 0)),
        ]
        out_shape = [jax.ShapeDtypeStruct((n_p, D_MODEL), F32), jax.ShapeDtypeStruct((n_s, D_MODEL), F32)]
    return pl.pallas_call(
        functools.partial(_combine_kernel, tm=tm, n_prompt_blocks=npb),
        grid_spec=pltpu.PrefetchScalarGridSpec(
            num_scalar_prefetch=1,
            grid=(n // tm,),
            in_specs=[
                pl.BlockSpec((tm, D_MODEL), row),
                pl.BlockSpec((tm, ROUTE_LANES), row),
                pl.BlockSpec(memory_space=pl.ANY),
                pl.BlockSpec((1, D_MODEL), const),
                pl.BlockSpec((1, D_MODEL), const),
            ],
            out_specs=out_specs,
            scratch_shapes=[pltpu.VMEM((2, 2, tm * TOKEN_ROWS, LANE), jnp.uint32), pltpu.SemaphoreType.DMA((2,))],
        ),
        out_shape=out_shape,
        compiler_params=_params(("arbitrary",)),
        name="moe_combine_ln" if split is None else "moe_combine_ln_final",
    )(pos, x1, rw, ys, g, b)


def kernel(x_prompt, x_sample, state_gla, cache_pool, ln_in_g, ln_in_b, w_in, w_forget_up, b_forget, gla_norm_g, w_pool, pool_scale, w_out, ln1_g, ln1_b, router_group_w, router_group_b, router_expert_w, router_expert_b, w_exp_gate, w_exp_up, w_exp_down, ln2_g, ln2_b):
    n_pb, seq, d = x_prompt.shape
    n_sb, dseq, _ = x_sample.shape
    n_p, n_s = n_pb * seq, n_sb * dseq
    depth = w_in.shape[0]
    row2 = lambda v: v.reshape(1, -1)

    w_out_b = w_out.astype(BF16)
    w_in_t = jnp.swapaxes(w_in, 1, 2)
    wfu = jnp.concatenate([w_forget_up, jnp.zeros((depth, LANE - GATE_RANK, GLA_KW), F32)], axis=1).astype(BF16)
    wpool = w_pool.astype(BF16)
    wr = jnp.concatenate(
        [router_group_w, router_expert_w.transpose(0, 2, 1, 3).reshape(depth, d, N_EXPERTS),
         jnp.zeros((depth, d, ROUTE_LANES - N_GROUPS - N_EXPERTS), F32)], axis=2)
    br = jnp.concatenate(
        [router_group_b, router_expert_b.reshape(depth, N_EXPERTS),
         jnp.zeros((depth, ROUTE_LANES - N_GROUPS - N_EXPERTS), F32)], axis=1)
    w1 = w_exp_gate.reshape(depth * N_EXPERTS, d, EXPERT_HIDDEN)
    w3 = w_exp_up.reshape(depth * N_EXPERTS, d, EXPERT_HIDDEN)
    w2 = w_exp_down.reshape(depth * N_EXPERTS, EXPERT_HIDDEN, d)
    hist0 = jnp.concatenate([jnp.zeros((depth, n_sb, 1, POOL_WIDTH), F32), cache_pool], axis=2)

    x, xb = _ln_in(x_prompt.reshape(n_p, d), x_sample.reshape(n_s, d), row2(ln_in_g), row2(ln_in_b))
    states_p, hists_p, states_s, hists_s = [], [], [], []
    for l in range(depth):
        qkvg = _in_proj_qkvg(xb, w_in_t, l)
        pa = _in_proj_pa(xb, w_in_t, l)
        mix_w = (wfu[l], row2(b_forget[l]), row2(gla_norm_g[l]), wpool[l], row2(pool_scale[l]))
        mixed, sp, hp = _mixer(qkvg, pa, *mix_w, row0=0, n_seq=n_pb, seq_len=seq, n_hist=0)
        mixed, ss, hs = _mixer(qkvg, pa, *mix_w, row0=n_p, n_seq=n_sb, seq_len=dseq, n_hist=POOL_HIST,
                               init=(state_gla[l], hist0[l]), mixed_in=mixed)
        x1, x1p, ri, rw = _out_proj(mixed, x, w_out_b, row2(ln1_g[l]), row2(ln1_b[l]), wr[l], row2(br[l]), l)
        pos, plan = _dispatch_plan(ri)
        xs = _dispatch(pos, x1p)
        ys = _experts(plan, xs, w1, w3, w2, l)
        if l + 1 < depth:
            x, xb = _combine(pos, x1, rw, ys, row2(ln2_g[l]), row2(ln2_b[l]))
        else:
            y_p, y_s = _combine(pos, x1, rw, ys, row2(ln2_g[l]), row2(ln2_b[l]), split=(n_p, n_s))
        states_p.append(sp)
        hists_p.append(hp[:, 1:])
        states_s.append(ss)
        hists_s.append(hs[:, 1:])
    return (y_p.reshape(n_pb, seq, d), y_s.reshape(n_sb, dseq, d),
            jnp.stack(states_p), jnp.stack(hists_p), jnp.stack(states_s), jnp.stack(hists_s))
```

```python
import functools

import jax
import jax.numpy as jnp
from jax import lax
from jax.experimental import pallas as pl
from jax.experimental.pallas import tpu as pltpu

D_MODEL = 2048
DEPTH = 4
CHUNK = 64

GLA_HEADS = 4
GLA_WIDTH = D_MODEL // 2
GLA_DV = GLA_WIDTH // GLA_HEADS
GLA_DK = GLA_DV // 2
GLA_KW = GLA_HEADS * GLA_DK
GATE_RANK = 16
GATE_TAU = 16.0

POOL_WINDOWS = (2, 4, 8, 16)
POOL_GROUPS = len(POOL_WINDOWS)
POOL_WIDTH = D_MODEL - GLA_WIDTH
POOL_GC = POOL_WIDTH // POOL_GROUPS
POOL_HIST = max(POOL_WINDOWS) - 1
HIST_ROWS = POOL_HIST + 1

Q_OFF = 0
K_OFF = Q_OFF + GLA_KW
V_OFF = K_OFF + GLA_KW
G_OFF = V_OFF + GLA_WIDTH
A_OFF = G_OFF + GLA_WIDTH
P_OFF = A_OFF + GATE_RANK
IN_COLS = P_OFF + POOL_WIDTH
PA_COLS = POOL_WIDTH + 128

N_GROUPS = 4
N_EXP = 8
N_EXPERTS = N_GROUPS * N_EXP
EXPERT_HIDDEN = D_MODEL // 4
ROUTE_LANES = 128
ROUTER_CORRECTION = 0.5

DN_ALPHA = (2 * DEPTH) ** 0.25
LN_EPS = 1e-5
RMS_EPS = 1e-6

VMEM_LIMIT_BYTES = 56 * 1024 * 1024
LANE = 128
TOKEN_ROWS = D_MODEL // 2 // LANE
EXPERT_TILE = 256
BF16 = jnp.bfloat16
F32 = jnp.float32


def _params(sem, vmem=VMEM_LIMIT_BYTES):
    return pltpu.CompilerParams(dimension_semantics=sem, vmem_limit_bytes=vmem)


def _row_tile(n, prefs=(640, 512, 320, 256, 128)):
    for t in prefs:
        if n % t == 0:
            return t
    raise ValueError(f"no row tile for {n} rows")


def _layer_norm(x, g, b):
    mu = jnp.mean(x, axis=-1, keepdims=True)
    xc = x - mu
    var = jnp.mean(xc * xc, axis=-1, keepdims=True)
    return xc * lax.rsqrt(var + LN_EPS) * g + b


def _dot(a, b):
    return jnp.dot(a, b, preferred_element_type=F32)


def _dot_nt(a, b):
    return lax.dot_general(a, b, (((1,), (1,)), ((), ())), preferred_element_type=F32)


def _pack_pairs(x):
    k = x.shape[1] // 2
    hi = lax.bitcast_convert_type(x[:, :k].astype(BF16).astype(F32), jnp.uint32)
    lo = lax.bitcast_convert_type(x[:, k:].astype(BF16).astype(F32), jnp.uint32)
    return hi | (lo >> 16)


def _unpack_pairs(w):
    hi = lax.bitcast_convert_type(w & jnp.uint32(0xFFFF0000), F32)
    lo = lax.bitcast_convert_type(w << 16, F32)
    return jnp.concatenate([hi, lo], axis=1)


def _store_token_tiles(ref, words):
    m = words.shape[0]
    for j in range(TOKEN_ROWS):
        ref[pl.ds(j, m, stride=TOKEN_ROWS), :] = words[:, j * LANE:(j + 1) * LANE]


def _load_token_tiles(ref, m):
    return jnp.concatenate([ref[pl.ds(j, m, stride=TOKEN_ROWS), :] for j in range(TOKEN_ROWS)], axis=1)


def _dot_split(m_bf16, x):
    hi = x.astype(BF16)
    lo = (x - hi.astype(F32)).astype(BF16)
    return _dot(m_bf16, hi) + _dot(m_bf16, lo)


def _ln_in_kernel(xp_ref, xs_ref, g_ref, b_ref, x_ref, xb_ref, *, n_prompt_blocks):
    i = pl.program_id(0)
    x = jnp.where(i < n_prompt_blocks, xp_ref[...], xs_ref[...])
    y = _layer_norm(x, g_ref[...], b_ref[...])
    x_ref[...] = y
    xb_ref[...] = y.astype(BF16)


def _ln_in(xp, xs, g, b):
    n_p, n_s = xp.shape[0], xs.shape[0]
    tm = n_s
    assert n_p % tm == 0
    npb = n_p // tm
    n = n_p + n_s
    return pl.pallas_call(
        functools.partial(_ln_in_kernel, n_prompt_blocks=npb),
        grid=(npb + 1,),
        in_specs=[
            pl.BlockSpec((tm, D_MODEL), lambda i: (jnp.minimum(i, npb - 1), 0)),
            pl.BlockSpec((tm, D_MODEL), lambda i: (0, 0)),
            pl.BlockSpec((1, D_MODEL), lambda i: (0, 0)),
            pl.BlockSpec((1, D_MODEL), lambda i: (0, 0)),
        ],
        out_specs=[
            pl.BlockSpec((tm, D_MODEL), lambda i: (i, 0)),
            pl.BlockSpec((tm, D_MODEL), lambda i: (i, 0)),
        ],
        out_shape=[jax.ShapeDtypeStruct((n, D_MODEL), F32), jax.ShapeDtypeStruct((n, D_MODEL), BF16)],
        compiler_params=_params(("arbitrary",)),
        name="ln_in",
    )(xp, xs, g, b)


def _proj_qkvg_kernel(x_ref, wt_ref, o_ref, wb_ref):
    @pl.when(pl.program_id(1) == 0)
    def _():
        wb_ref[...] = wt_ref[...].astype(BF16)

    o_ref[...] = _dot_nt(x_ref[...], wb_ref[...]).astype(o_ref.dtype)


def _in_proj_qkvg(xb, w_in_t, layer):
    n = xb.shape[0]
    tm, tn = _row_tile(n, (1664, 640, 512, 384, 256, 128)), 1024
    return pl.pallas_call(
        _proj_qkvg_kernel,
        grid=(A_OFF // tn, n // tm),
        in_specs=[
            pl.BlockSpec((tm, D_MODEL), lambda j, m: (m, 0)),
            pl.BlockSpec((None, tn, D_MODEL), lambda j, m: (layer, j, 0)),
        ],
        out_specs=pl.BlockSpec((tm, tn), lambda j, m: (m, j)),
        out_shape=jax.ShapeDtypeStruct((n, A_OFF), BF16),
        scratch_shapes=[pltpu.VMEM((tn, D_MODEL), BF16)],
        compiler_params=_params(("arbitrary", "arbitrary")),
        name="in_proj_qkvg",
    )(xb, w_in_t)


def _proj_pa_kernel(x_ref, wt_hbm, o_ref, stage_ref, wb_ref, sem, *, layer):
    n_tail = IN_COLS - A_OFF

    @pl.when(pl.program_id(0) == 0)
    def _():
        copies = (
            pltpu.make_async_copy(wt_hbm.at[layer, pl.ds(P_OFF, POOL_WIDTH)], stage_ref.at[pl.ds(0, POOL_WIDTH)],
                                  sem.at[0]),
            pltpu.make_async_copy(wt_hbm.at[layer, pl.ds(A_OFF, GATE_RANK)],
                                  stage_ref.at[pl.ds(POOL_WIDTH, GATE_RANK)], sem.at[1]),
        )
        for c in copies:
            c.start()
        for c in copies:
            c.wait()
        wb_ref[0:n_tail, :] = stage_ref[...].astype(BF16)
        wb_ref[n_tail:, :] = jnp.zeros((PA_COLS - n_tail, D_MODEL), BF16)

    o_ref[...] = _dot_nt(x_ref[...], wb_ref[...])


def _in_proj_pa(xb, w_in_t, layer):
    n = xb.shape[0]
    tm = _row_tile(n, (1664, 640, 512, 384, 256, 128))
    return pl.pallas_call(
        functools.partial(_proj_pa_kernel, layer=layer),
        grid=(n // tm,),
        in_specs=[
            pl.BlockSpec((tm, D_MODEL), lambda m: (m, 0)),
            pl.BlockSpec(memory_space=pl.ANY),
        ],
        out_specs=pl.BlockSpec((tm, PA_COLS), lambda m: (m, 0)),
        out_shape=jax.ShapeDtypeStruct((n, PA_COLS), F32),
        scratch_shapes=[
            pltpu.VMEM((IN_COLS - A_OFF, D_MODEL), F32),
            pltpu.VMEM((PA_COLS, D_MODEL), BF16),
            pltpu.SemaphoreType.DMA((2,)),
        ],
        compiler_params=_params(("arbitrary",)),
        name="in_proj_pa",
    )(xb, w_in_t)


def _mixer_kernel(*refs, n_seq, n_blk, n_chunks, valid, n_hist, has_init, fill_tail):
    if has_init:
        (qkvg_ref, pa_ref, wfu_ref, bf_ref, gain_ref, wpool_ref, pscale_ref, s0_ref, h0_ref, _mixed_in,
         mixed_ref, sout_ref, hout_ref, s_scr, z_scr) = refs
    else:
        (qkvg_ref, pa_ref, wfu_ref, bf_ref, gain_ref, wpool_ref, pscale_ref,
         mixed_ref, sout_ref, hout_ref, s_scr, z_scr) = refs
    C = CHUNK
    step = pl.program_id(0)
    blk = step % n_blk

    def pad_rows(x):
        if valid == C:
            return x
        return jnp.concatenate([x, jnp.zeros((C - valid, x.shape[1]), x.dtype)], axis=0)

    def process():
        @pl.when(blk == 0)
        def _():
            if has_init:
                s_scr[...] = s0_ref[...]
                z_scr[0:C - HIST_ROWS, :] = jnp.zeros((C - HIST_ROWS, POOL_WIDTH), F32)
                z_scr[C - HIST_ROWS:C, :] = h0_ref[...]
            else:
                s_scr[...] = jnp.zeros(s_scr.shape, F32)
                z_scr[...] = jnp.zeros(z_scr.shape, F32)

        row = lax.broadcasted_iota(jnp.int32, (C, C), 0)
        col = lax.broadcasted_iota(jnp.int32, (C, C), 1)
        causal = row >= col
        tri = causal.astype(BF16)
        t_idx = lax.broadcasted_iota(jnp.int32, (C, 2 * C), 0)
        s_idx = lax.broadcasted_iota(jnp.int32, (C, 2 * C), 1) - C
        lag = t_idx - s_idx
        bands = [((lag >= 0) & (lag < w)).astype(BF16) for w in POOL_WINDOWS]

        def chunk(j, carry):
            if n_chunks == 1:
                rows = slice(0, valid)
            else:
                rows = pl.ds(pl.multiple_of(j * C, C), C)
            qkvg = pad_rows(qkvg_ref[rows, :])
            pa = pad_rows(pa_ref[rows, :])
            q = qkvg[:, Q_OFF:K_OFF].astype(F32)
            k = qkvg[:, K_OFF:V_OFF].astype(F32)
            v = qkvg[:, V_OFF:G_OFF]
            g = qkvg[:, G_OFF:A_OFF].astype(F32)
            zc = pa[:, :POOL_WIDTH]
            a = pa[:, POOL_WIDTH:]

            la = jax.nn.log_sigmoid(_dot(a.astype(BF16), wfu_ref[...]) + bf_ref[...]) * (1.0 / GATE_TAU)
            if valid != C:
                la = jnp.where(lax.broadcasted_iota(jnp.int32, la.shape, 0) < valid, la, 0.0)
            bcum = _dot_split(tri, la)
            b_last = bcum[C - 1:C, :]
            q_t = (q * jnp.exp(bcum) * (GLA_DK ** -0.5)).astype(BF16)
            k_t = (k * jnp.exp(-bcum)).astype(BF16)
            k_dec = k * jnp.exp(b_last - bcum)
            e_last = jnp.exp(b_last)
            outs = []
            for h in range(GLA_HEADS):
                ks = slice(h * GLA_DK, (h + 1) * GLA_DK)
                vs = slice(h * GLA_DV, (h + 1) * GLA_DV)
                s_h = s_scr[h]
                att = jnp.where(causal, _dot_nt(q_t[:, ks], k_t[:, ks]), 0.0)
                o = _dot(att.astype(BF16), v[:, vs]) + _dot(q_t[:, ks], s_h.astype(BF16))
                k_aug = jnp.concatenate([k_dec[:, ks], jnp.zeros((GLA_DK - C, GLA_DK), F32)], axis=0)
                v_aug = jnp.concatenate([v[:, vs], jnp.zeros((GLA_DK - C, GLA_DV), BF16)], axis=0)
                decay = jnp.broadcast_to(e_last[:, ks], (GLA_DK, GLA_DK)).T
                decay = jnp.concatenate([decay] * (GLA_DV // GLA_DK), axis=1)
                s_scr[h] = decay * s_h + _dot(k_aug.T.astype(BF16), v_aug)
                outs.append(o * lax.rsqrt(jnp.mean(o * o, axis=-1, keepdims=True) + RMS_EPS))
            gla = jnp.concatenate(outs, axis=1) * gain_ref[...] * (g * jax.nn.sigmoid(g))

            z_ext = jnp.concatenate([z_scr[...], zc], axis=0)
            t0 = (blk * n_chunks + j) * C + n_hist + 1
            avail = lax.broadcasted_iota(jnp.int32, (C, POOL_GC), 0) + t0
            pooled = []
            for gi, w in enumerate(POOL_WINDOWS):
                cs = slice(gi * POOL_GC, (gi + 1) * POOL_GC)
                cnt = jnp.minimum(avail, w).astype(F32)
                m = _dot_split(bands[gi], z_ext[:, cs]) / cnt - zc[:, cs]
                pooled.append(_dot(m.astype(BF16), wpool_ref[gi]))
            pool = jnp.concatenate(pooled, axis=1) * pscale_ref[...]
            z_scr[...] = zc

            res = jnp.concatenate([gla, pool], axis=1).astype(BF16)
            mixed_ref[rows, :] = res[0:valid]
            return carry

        if n_chunks == 1:
            chunk(0, 0)
        else:
            lax.fori_loop(0, n_chunks, chunk, 0, unroll=2)

        @pl.when(blk == n_blk - 1)
        def _():
            sout_ref[...] = s_scr[...]
            hout_ref[...] = z_scr[valid - HIST_ROWS:valid, :]

    if fill_tail:
        pl.when(step < n_seq * n_blk)(process)

        @pl.when(step == n_seq * n_blk)
        def _():
            mixed_ref[...] = jnp.zeros(mixed_ref.shape, BF16)
    else:
        process()


def _mixer_block_kernel(*refs, n_seq, n_blk, rows_valid, n_hist, has_init, fill_tail):
    if has_init:
        (qkvg_ref, pa_ref, wfu_ref, bf_ref, gain_ref, wpool_ref, pscale_ref, s0_ref, h0_ref, _mixed_in,
         mixed_ref, sout_ref, hout_ref, s_scr, z_scr) = refs
    else:
        (qkvg_ref, pa_ref, wfu_ref, bf_ref, gain_ref, wpool_ref, pscale_ref,
         mixed_ref, sout_ref, hout_ref, s_scr, z_scr) = refs
    C = CHUNK
    R = max(rows_valid, 2 * C)
    nc = R // C
    step = pl.program_id(0)
    blk = step % n_blk

    def pad_rows(x):
        if rows_valid == R:
            return x
        return jnp.concatenate([x, jnp.zeros((R - rows_valid, x.shape[1]), x.dtype)], axis=0)

    def process():
        @pl.when(blk == 0)
        def _():
            if has_init:
                s_scr[...] = s0_ref[...]
                z_scr[0:C - HIST_ROWS, :] = jnp.zeros((C - HIST_ROWS, POOL_WIDTH), F32)
                z_scr[C - HIST_ROWS:C, :] = h0_ref[...]
            else:
                s_scr[...] = jnp.zeros(s_scr.shape, F32)
                z_scr[...] = jnp.zeros(z_scr.shape, F32)

        qkvg = pad_rows(qkvg_ref[...])
        pa = pad_rows(pa_ref[...])
        q = qkvg[:, Q_OFF:K_OFF].astype(F32)
        k = qkvg[:, K_OFF:V_OFF].astype(F32)
        v = qkvg[:, V_OFF:G_OFF]
        g = qkvg[:, G_OFF:A_OFF].astype(F32)
        zc = pa[:, :POOL_WIDTH]
        a = pa[:, POOL_WIDTH:]

        row = lax.broadcasted_iota(jnp.int32, (R, R), 0)
        col = lax.broadcasted_iota(jnp.int32, (R, R), 1)
        causal = (row // C == col // C) & (row >= col)
        la = jax.nn.log_sigmoid(_dot(a.astype(BF16), wfu_ref[...]) + bf_ref[...]) * (1.0 / GATE_TAU)
        if rows_valid != R:
            la = jnp.where(lax.broadcasted_iota(jnp.int32, la.shape, 0) < rows_valid, la, 0.0)
        bcum = _dot_split(causal.astype(BF16), la)
        tot = bcum.reshape(nc, C, GLA_KW)[:, C - 1:C, :]
        tot = jnp.broadcast_to(tot, (nc, C, GLA_KW)).reshape(R, GLA_KW)
        q_t = (q * jnp.exp(bcum) * (GLA_DK ** -0.5)).astype(BF16)
        k_t = (k * jnp.exp(-bcum)).astype(BF16)
        k_dec = k * jnp.exp(tot - bcum)
        e_tot = jnp.exp(tot)
        first_half = lax.broadcasted_iota(jnp.int32, (GLA_DK, 2 * C), 1) < C
        outs = []
        for h in range(GLA_HEADS):
            ks = slice(h * GLA_DK, (h + 1) * GLA_DK)
            vs = slice(h * GLA_DV, (h + 1) * GLA_DV)
            att = jnp.where(causal, _dot_nt(q_t[:, ks], k_t[:, ks]), 0.0)
            o_intra = _dot(att.astype(BF16), v[:, vs])
            k_dec_t = k_dec[:, ks].T
            e_tot_t = e_tot[:, ks].T
            updates = []
            for m in range(nc // 2):
                kt_pair = k_dec_t[:, 2 * C * m:2 * C * (m + 1)]
                v_pair = v[2 * C * m:2 * C * (m + 1), vs]
                updates.append(_dot(jnp.where(first_half, kt_pair, 0.0).astype(BF16), v_pair))
                updates.append(_dot(jnp.where(first_half, 0.0, kt_pair).astype(BF16), v_pair))
            s_h = s_scr[h]
            o_inter = []
            for j in range(nc):
                o_inter.append(_dot(q_t[C * j:C * (j + 1), ks], s_h.astype(BF16)))
                decay = jnp.broadcast_to(e_tot_t[:, C * j:C * j + 1], (GLA_DK, GLA_DV))
                s_h = decay * s_h + updates[j]
            s_scr[h] = s_h
            o = o_intra + jnp.concatenate(o_inter, axis=0)
            outs.append(o * lax.rsqrt(jnp.mean(o * o, axis=-1, keepdims=True) + RMS_EPS))
        gla = jnp.concatenate(outs, axis=1) * gain_ref[...] * (g * jax.nn.sigmoid(g))

        z_all = jnp.concatenate([z_scr[...], zc], axis=0)
        z_hi = z_all.astype(BF16)
        z_lo = (z_all - z_hi.astype(F32)).astype(BF16)
        t_idx = lax.broadcasted_iota(jnp.int32, (C, 2 * C), 0)
        s_idx = lax.broadcasted_iota(jnp.int32, (C, 2 * C), 1) - C
        lag = t_idx - s_idx
        avail0 = lax.broadcasted_iota(jnp.int32, (C, POOL_GC), 0) + (blk * rows_valid + n_hist + 1)
        pooled = []
        for gi, w in enumerate(POOL_WINDOWS):
            cs = slice(gi * POOL_GC, (gi + 1) * POOL_GC)
            band = ((lag >= 0) & (lag < w)).astype(BF16)
            means = []
            for j in range(nc):
                win = slice(C * j, C * (j + 2))
                sums = _dot(band, z_hi[win, cs]) + _dot(band, z_lo[win, cs])
                cnt = jnp.minimum(avail0 + C * j, w).astype(F32)
                means.append(sums / cnt - zc[C * j:C * (j + 1), cs])
            pooled.append(_dot(jnp.concatenate(means, axis=0).astype(BF16), wpool_ref[gi]))
        pool = jnp.concatenate(pooled, axis=1) * pscale_ref[...]
        z_scr[...] = zc[R - C:R, :]

        res = jnp.concatenate([gla, pool], axis=1).astype(BF16)
        mixed_ref[...] = res[0:rows_valid]

        @pl.when(blk == n_blk - 1)
        def _():
            sout_ref[...] = s_scr[...]
            hout_ref[...] = zc[rows_valid - HIST_ROWS:rows_valid, :]

    if fill_tail:
        pl.when(step < n_seq * n_blk)(process)

        @pl.when(step == n_seq * n_blk)
        def _():
            mixed_ref[...] = jnp.zeros(mixed_ref.shape, BF16)
    else:
        process()


def _mixer(qkvg, pa, wfu, bfg, gain, wpool, pscale, *, row0, n_seq, seq_len, n_hist, init=None, mixed_in=None):
    n = qkvg.shape[0]
    if seq_len % (2 * CHUNK) == 0:
        rb = _row_tile(seq_len, (512, 256, 128))
    else:
        rb = seq_len
        assert HIST_ROWS <= seq_len < CHUNK and init is not None
    assert seq_len % rb == 0 and row0 % rb == 0
    n_blk = seq_len // rb
    base = row0 // rb
    n_steps = n_seq * n_blk
    has_init = init is not None
    assert has_init == (mixed_in is not None)
    tail_rows = n - (row0 + n_seq * seq_len)
    fill_tail = (not has_init) and tail_rows > 0
    assert tail_rows <= rb

    rows = lambda t: (base + t, 0)
    const2 = lambda t: (0, 0)
    seq3 = lambda t: (jnp.minimum(t // n_blk, n_seq - 1), 0, 0)
    seq4 = lambda t: (jnp.minimum(t // n_blk, n_seq - 1), 0, 0, 0)

    in_specs = [
        pl.BlockSpec((rb, A_OFF), rows),
        pl.BlockSpec((rb, PA_COLS), rows),
        pl.BlockSpec((LANE, GLA_KW), const2),
        pl.BlockSpec((1, GLA_KW), const2),
        pl.BlockSpec((1, GLA_WIDTH), const2),
        pl.BlockSpec((POOL_GROUPS, POOL_GC, POOL_GC), lambda t: (0, 0, 0)),
        pl.BlockSpec((1, POOL_WIDTH), const2),
    ]
    args = [qkvg, pa, wfu, bfg, gain, wpool, pscale]
    aliases = {}
    if has_init:
        in_specs += [
            pl.BlockSpec((None, GLA_HEADS, GLA_DK, GLA_DV), seq4),
            pl.BlockSpec((None, HIST_ROWS, POOL_WIDTH), seq3),
            pl.BlockSpec(memory_space=pl.ANY),
        ]
        args += [init[0], init[1], mixed_in]
        aliases = {len(args) - 1: 0}
    return pl.pallas_call(
        functools.partial(_mixer_block_kernel, n_seq=n_seq, n_blk=n_blk, rows_valid=rb, n_hist=n_hist,
                          has_init=has_init, fill_tail=fill_tail),
        grid=(n_steps + int(fill_tail),),
        in_specs=in_specs,
        out_specs=[
            pl.BlockSpec((rb, D_MODEL), rows),
            pl.BlockSpec((None, GLA_HEADS, GLA_DK, GLA_DV), seq4),
            pl.BlockSpec((None, HIST_ROWS, POOL_WIDTH), seq3),
        ],
        out_shape=[
            jax.ShapeDtypeStruct((n, D_MODEL), BF16),
            jax.ShapeDtypeStruct((n_seq, GLA_HEADS, GLA_DK, GLA_DV), F32),
            jax.ShapeDtypeStruct((n_seq, HIST_ROWS, POOL_WIDTH), F32),
        ],
        scratch_shapes=[
            pltpu.VMEM((GLA_HEADS, GLA_DK, GLA_DV), F32),
            pltpu.VMEM((CHUNK, POOL_WIDTH), F32),
        ],
        input_output_aliases=aliases,
        compiler_params=_params(("arbitrary",)),
        name="mixer_init" if has_init else "mixer",
    )(*args)


def _route(logits):
    lane = lax.broadcasted_iota(jnp.int32, logits.shape, 1).astype(F32)
    neg = -jnp.inf
    first_lane = lambda hit: jnp.min(jnp.where(hit, lane, float(ROUTE_LANES)), axis=-1, keepdims=True)
    lg = jnp.where(lane < N_GROUPS, logits, neg)
    un = jnp.exp(lg - jnp.max(lg, axis=-1, keepdims=True))
    pg = un / jnp.sum(un, axis=-1, keepdims=True)
    top_pg = jnp.max(pg, axis=-1, keepdims=True)
    gsel = first_lane((pg == top_pg) & (lane < N_GROUPS))
    first = N_GROUPS + gsel * N_EXP
    le = jnp.where((lane >= first) & (lane < first + N_EXP), logits, neg)
    v0 = jnp.max(le, axis=-1, keepdims=True)
    i0 = first_lane(le == v0)
    le = jnp.where(lane == i0, neg, le)
    v1 = jnp.max(le, axis=-1, keepdims=True)
    i1 = first_lane(le == v1)
    u1 = jnp.exp(v1 - v0)
    den = 1.0 + u1
    e0 = (i0 - N_GROUPS).astype(jnp.int32)
    e1 = (i1 - N_GROUPS).astype(jnp.int32)
    return e0, e1, (1.0 / den) * top_pg, (u1 / den) * top_pg


def _out_proj_kernel(mixed_ref, x_ref, w_ref, g_ref, b_ref, wr_ref, br_ref, x1_ref, x1p_ref, ri_ref, rw_ref,
                     wrs_ref):
    tm = x_ref.shape[0]

    @pl.when(pl.program_id(0) == 0)
    def _():
        wr = wr_ref[...]
        wr_hi = wr.astype(BF16)
        wrs_ref[:, :ROUTE_LANES] = wr_hi
        wrs_ref[:, ROUTE_LANES:] = (wr - wr_hi.astype(F32)).astype(BF16)

    y = _dot(mixed_ref[...], w_ref[...])
    x1 = _layer_norm(DN_ALPHA * x_ref[...] + y, g_ref[...], b_ref[...])
    x1_ref[...] = x1
    _store_token_tiles(x1p_ref, _pack_pairs(x1))
    x_hi = x1.astype(BF16)
    x_lo = (x1 - x_hi.astype(F32)).astype(BF16)
    prod = _dot(jnp.concatenate([x_hi, x_lo], axis=0), wrs_ref[...])
    corr = prod[tm:, :ROUTE_LANES] + prod[:tm, ROUTE_LANES:]
    logits = prod[:tm, :ROUTE_LANES] + ROUTER_CORRECTION * corr + br_ref[...]
    e0, e1, w0, w1 = _route(logits)
    lane = lax.broadcasted_iota(jnp.int32, logits.shape, 1)
    ri_ref[...] = jnp.where(lane == 0, e0, jnp.where(lane == 1, e1, 0))
    rw_ref[...] = jnp.where(lane == 0, w0, jnp.where(lane == 1, w1, 0.0))


def _out_proj(mixed, x, w_out, g, b, wr, br, layer):
    n = x.shape[0]
    tm = _row_tile(n, (640, 416, 320, 256, 128))
    row = lambda i: (i, 0)
    const = lambda i: (0, 0)
    return pl.pallas_call(
        _out_proj_kernel,
        grid=(n // tm,),
        in_specs=[
            pl.BlockSpec((tm, D_MODEL), row),
            pl.BlockSpec((tm, D_MODEL), row),
            pl.BlockSpec((None, D_MODEL, D_MODEL), lambda i: (layer, 0, 0), pipeline_mode=pl.Buffered(1)),
            pl.BlockSpec((1, D_MODEL), const),
            pl.BlockSpec((1, D_MODEL), const),
            pl.BlockSpec((D_MODEL, ROUTE_LANES), const),
            pl.BlockSpec((1, ROUTE_LANES), const),
        ],
        out_specs=[
            pl.BlockSpec((tm, D_MODEL), row),
            pl.BlockSpec((tm * TOKEN_ROWS, LANE), row),
            pl.BlockSpec((tm, ROUTE_LANES), row),
            pl.BlockSpec((tm, ROUTE_LANES), row),
        ],
        out_shape=[
            jax.ShapeDtypeStruct((n, D_MODEL), F32),
            jax.ShapeDtypeStruct((n * TOKEN_ROWS, LANE), jnp.uint32),
            jax.ShapeDtypeStruct((n, ROUTE_LANES), jnp.int32),
            jax.ShapeDtypeStruct((n, ROUTE_LANES), F32),
        ],
        scratch_shapes=[pltpu.VMEM((D_MODEL, 2 * ROUTE_LANES), BF16)],
        compiler_params=_params(("arbitrary",)),
        name="out_proj_ln_route",
    )(mixed, x, w_out, g, b, wr, br)


def _sort_kernel(ri_ref, pos_ref, cnt_ref, *, tb):
    n = ri_ref.shape[0]
    n_blocks = n // tb
    lane = lax.broadcasted_iota(jnp.int32, (tb, ROUTE_LANES), 1)
    ones = jnp.ones((8, tb), BF16)

    def onehots(b):
        ri = ri_ref[pl.ds(pl.multiple_of(b * tb, tb), tb), :]
        return lane == ri[:, 0:1], lane == ri[:, 1:2]

    def count(b, acc):
        h0, h1 = onehots(b)
        return acc + _dot(ones, jnp.where(h0 | h1, 1.0, 0.0).astype(BF16))

    counts = lax.fori_loop(0, n_blocks, count, jnp.zeros((8, ROUTE_LANES), F32))
    cnt_ref[...] = counts.astype(jnp.int32)
    before = (lax.broadcasted_iota(jnp.int32, (ROUTE_LANES, ROUTE_LANES), 0)
              < lax.broadcasted_iota(jnp.int32, (ROUTE_LANES, ROUTE_LANES), 1)).astype(BF16)
    c_hi = jnp.floor(counts * (1.0 / ROUTE_LANES))
    c_lo = counts - c_hi * ROUTE_LANES
    offs = (_dot(c_hi.astype(BF16), before) * ROUTE_LANES + _dot(c_lo.astype(BF16), before))[0:1, :]
    earlier = (lax.broadcasted_iota(jnp.int32, (tb, tb), 0) > lax.broadcasted_iota(jnp.int32, (tb, tb), 1)).astype(BF16)

    def place(b, carry):
        h0, h1 = onehots(b)
        both = jnp.where(h0 | h1, 1.0, 0.0).astype(BF16)
        start = _dot(earlier, both) + carry + offs
        p0 = jnp.sum(jnp.where(h0, start, 0.0), axis=-1, keepdims=True)
        p1 = jnp.sum(jnp.where(h1, start, 0.0), axis=-1, keepdims=True)
        out = jnp.where(lane == 0, p0, jnp.where(lane == 1, p1, 0.0))
        pos_ref[pl.ds(pl.multiple_of(b * tb, tb), tb), :] = out.astype(jnp.int32)
        return carry + _dot(ones, both)[0:1, :]

    lax.fori_loop(0, n_blocks, place, jnp.zeros((1, ROUTE_LANES), F32))


def _sort_pairs(ri):
    n = ri.shape[0]
    tb = _row_tile(n, (640, 512, 384, 256, 128))
    pos, cnt = pl.pallas_call(
        functools.partial(_sort_kernel, tb=tb),
        out_shape=[jax.ShapeDtypeStruct((n, ROUTE_LANES), jnp.int32), jax.ShapeDtypeStruct((8, ROUTE_LANES), jnp.int32)],
        compiler_params=pltpu.CompilerParams(vmem_limit_bytes=VMEM_LIMIT_BYTES),
        name="moe_sort",
    )(ri)
    return pos, cnt[0, :N_EXPERTS]


def _dispatch_plan(ri):
    n_pairs = ri.shape[0] * 2
    n_tiles = n_pairs // EXPERT_TILE
    pos, counts = _sort_pairs(ri)
    pos = pos[:, :2].reshape(-1)
    offs = jnp.concatenate([jnp.zeros((1,), jnp.int32), jnp.cumsum(counts)])
    bounds = jnp.sort(jnp.concatenate([jnp.arange(n_tiles, dtype=jnp.int32) * EXPERT_TILE, offs[:N_EXPERTS]]))
    seg_lo = bounds
    seg_hi = jnp.concatenate([bounds[1:], jnp.full((1,), n_pairs, jnp.int32)])
    tile = jnp.minimum(seg_lo // EXPERT_TILE, n_tiles - 1)
    expert = jnp.minimum(jnp.sum((offs[None, 1:] <= seg_lo[:, None]).astype(jnp.int32), axis=1), N_EXPERTS - 1)
    lo = seg_lo - tile * EXPERT_TILE
    hi = jnp.where(seg_hi > seg_lo, seg_hi - tile * EXPERT_TILE, lo)
    valid = seg_hi > seg_lo
    e_seen = lax.cummax(jnp.where(valid, expert, -1))
    prev = jnp.concatenate([jnp.full((1,), -1, jnp.int32), e_seen[:-1]])
    fetch = valid & (expert > prev)
    run = jnp.maximum(jnp.cumsum(fetch.astype(jnp.int32)) - 1, 0)
    slot = run % 2
    fetch = fetch.astype(jnp.int32) * jnp.where(run == 0, 2, 1)
    ids = jnp.arange(N_EXPERTS, dtype=jnp.int32)
    later = lax.cummin(jnp.where(counts > 0, ids, N_EXPERTS), reverse=True)
    nxt_of = jnp.concatenate([later[1:], jnp.full((1,), N_EXPERTS, jnp.int32)])
    nxt = jnp.where(nxt_of[expert] < N_EXPERTS, nxt_of[expert], -1)
    sched = (fetch, slot.astype(jnp.int32), nxt.astype(jnp.int32))
    return pos, (tile.astype(jnp.int32), expert, lo.astype(jnp.int32), hi.astype(jnp.int32)) + sched


def _dispatch_kernel(pos_ref, x_ref, xs_hbm, sem, *, tb):
    base = pl.program_id(0) * tb

    def scatter(t, k):
        dst = pl.multiple_of(pos_ref[2 * (base + t) + k] * TOKEN_ROWS, TOKEN_ROWS)
        src = pl.multiple_of(t * TOKEN_ROWS, TOKEN_ROWS)
        return pltpu.make_async_copy(x_ref.at[pl.ds(src, TOKEN_ROWS)], xs_hbm.at[pl.ds(dst, TOKEN_ROWS)], sem)

    def issue(t, c):
        scatter(t, 0).start(priority=0)
        scatter(t, 1).start(priority=1)
        return c

    lax.fori_loop(0, tb, issue, 0, unroll=8)

    def drain(t, c):
        scatter(t, 0).wait()
        scatter(t, 1).wait()
        return c

    lax.fori_loop(0, tb, drain, 0, unroll=8)


def _dispatch(pos, x1):
    n = x1.shape[0] // TOKEN_ROWS
    tb = _row_tile(n)
    return pl.pallas_call(
        functools.partial(_dispatch_kernel, tb=tb),
        grid_spec=pltpu.PrefetchScalarGridSpec(
            num_scalar_prefetch=1,
            grid=(n // tb,),
            in_specs=[pl.BlockSpec((tb * TOKEN_ROWS, LANE), lambda i, p: (i, 0))],
            out_specs=pl.BlockSpec(memory_space=pl.ANY),
            scratch_shapes=[pltpu.SemaphoreType.DMA(())],
        ),
        out_shape=jax.ShapeDtypeStruct((2 * n * TOKEN_ROWS, LANE), x1.dtype),
        compiler_params=_params(("arbitrary",)),
        name="moe_dispatch",
    )(pos, x1)


def _experts_kernel(tile_ref, exp_ref, lo_ref, hi_ref, fetch_ref, slot_ref, nxt_ref,
                    xs_ref, w1_hbm, w3_hbm, w2_hbm, o_ref, w1_buf, w3_buf, w2_buf, sem, *, layer):
    i = pl.program_id(0)
    lo, hi = lo_ref[i], hi_ref[i]
    slot = slot_ref[i]

    def weight_copies(expert, s):
        e = layer * N_EXPERTS + expert
        return (pltpu.make_async_copy(w1_hbm.at[e], w1_buf.at[s], sem.at[s, 0]),
                pltpu.make_async_copy(w3_hbm.at[e], w3_buf.at[s], sem.at[s, 1]),
                pltpu.make_async_copy(w2_hbm.at[e], w2_buf.at[s], sem.at[s, 2]))

    @pl.when(fetch_ref[i] == 2)
    def _():
        for c in weight_copies(exp_ref[i], slot):
            c.start()

    @pl.when(fetch_ref[i] > 0)
    def _():
        for c in weight_copies(exp_ref[i], slot):
            c.wait()

        @pl.when(nxt_ref[i] >= 0)
        def _():
            for c in weight_copies(nxt_ref[i], 1 - slot):
                c.start()

    def compute():
        x = _unpack_pairs(_load_token_tiles(xs_ref, EXPERT_TILE)).astype(BF16)
        h1 = _dot(x, w1_buf[slot].astype(BF16))
        h3 = _dot(x, w3_buf[slot].astype(BF16))
        a = (h1 * jax.nn.sigmoid(h1) * h3).astype(BF16)
        return _pack_pairs(_dot(a, w2_buf[slot].astype(BF16)))

    @pl.when((hi > lo) & (lo == 0))
    def _():
        _store_token_tiles(o_ref, compute())

    @pl.when((hi > lo) & (lo > 0))
    def _():
        r = lax.broadcasted_iota(jnp.int32, (EXPERT_TILE, D_MODEL // 2), 0)
        old = _load_token_tiles(o_ref, EXPERT_TILE)
        _store_token_tiles(o_ref, jnp.where((r >= lo) & (r < hi), compute(), old))


def _experts(plan, xs, w1, w3, w2, layer):
    n_items = plan[0].shape[0]
    xrow = lambda i, t, *_: (t[i], 0)
    any_spec = pl.BlockSpec(memory_space=pl.ANY)
    return pl.pallas_call(
        functools.partial(_experts_kernel, layer=layer),
        grid_spec=pltpu.PrefetchScalarGridSpec(
            num_scalar_prefetch=len(plan),
            grid=(n_items,),
            in_specs=[pl.BlockSpec((EXPERT_TILE * TOKEN_ROWS, LANE), xrow), any_spec, any_spec, any_spec],
            out_specs=pl.BlockSpec((EXPERT_TILE * TOKEN_ROWS, LANE), xrow),
            scratch_shapes=[
                pltpu.VMEM((2, D_MODEL, EXPERT_HIDDEN), F32),
                pltpu.VMEM((2, D_MODEL, EXPERT_HIDDEN), F32),
                pltpu.VMEM((2, EXPERT_HIDDEN, D_MODEL), F32),
                pltpu.SemaphoreType.DMA((2, 3)),
            ],
        ),
        out_shape=jax.ShapeDtypeStruct(xs.shape, xs.dtype),
        compiler_params=_params(("arbitrary",)),
        name="moe_experts",
    )(*plan, xs, w1, w3, w2)


def _combine_kernel(pos_ref, x1_ref, rw_ref, ys_hbm, g_ref, b_ref, *rest, tm, n_prompt_blocks):
    if n_prompt_blocks is None:
        x2_ref, x2b_ref, buf, sem = rest
    else:
        yp_ref, ysm_ref, buf, sem = rest
    i = pl.program_id(0)
    slot = i % 2

    def gather(step, s, t, k):
        src = pl.multiple_of(pos_ref[2 * (step * tm + t) + k] * TOKEN_ROWS, TOKEN_ROWS)
        dst = pl.multiple_of(t * TOKEN_ROWS, TOKEN_ROWS)
        return pltpu.make_async_copy(ys_hbm.at[pl.ds(src, TOKEN_ROWS)], buf.at[s, k, pl.ds(dst, TOKEN_ROWS)],
                                     sem.at[s])

    def issue(step, s):
        def body(t, c):
            gather(step, s, t, 0).start(priority=0)
            gather(step, s, t, 1).start(priority=1)
            return c

        lax.fori_loop(0, tm, body, 0, unroll=8)

    @pl.when(i == 0)
    def _():
        issue(0, 0)

    @pl.when(i + 1 < pl.num_programs(0))
    def _():
        issue(i + 1, 1 - slot)

    def drain(t, c):
        gather(i, slot, t, 0).wait()
        gather(i, slot, t, 1).wait()
        return c

    lax.fori_loop(0, tm, drain, 0, unroll=8)

    rw = rw_ref[...]
    f = (rw[:, 0:1] * _unpack_pairs(_load_token_tiles(buf.at[slot, 0], tm))
         + rw[:, 1:2] * _unpack_pairs(_load_token_tiles(buf.at[slot, 1], tm)))
    x2 = _layer_norm(DN_ALPHA * x1_ref[...] + f, g_ref[...], b_ref[...])
    if n_prompt_blocks is None:
        x2_ref[...] = x2
        x2b_ref[...] = x2.astype(BF16)
    else:
        @pl.when(i < n_prompt_blocks)
        def _():
            yp_ref[...] = x2

        @pl.when(i >= n_prompt_blocks)
        def _():
            ysm_ref[...] = x2


def _combine(pos, x1, rw, ys, g, b, *, split=None):
    n = x1.shape[0]
    row = lambda i, p: (i, 0)
    const = lambda i, p: (0, 0)
    if split is None:
        tm, npb = _row_tile(n, (640, 320, 256, 128)), None
        out_specs = [pl.BlockSpec((tm, D_MODEL), row), pl.BlockSpec((tm, D_MODEL), row)]
        out_shape = [jax.ShapeDtypeStruct((n, D_MODEL), F32), jax.ShapeDtypeStruct((n, D_MODEL), BF16)]
    else:
        n_p, n_s = split
        tm = n_s
        assert n_p % tm == 0 and n_p + n_s == n
        npb = n_p // tm
        out_specs = [
            pl.BlockSpec((tm, D_MODEL), lambda i, p: (jnp.minimum(i, npb - 1), 0)),
            pl.BlockSpec((tm, D_MODEL), lambda i, p: (jnp.maximum(i - npb, 0), 0)),
        ]
        out_shape = [jax.ShapeDtypeStruct((n_p, D_MODEL), F32), jax.ShapeDtypeStruct((n_s, D_MODEL), F32)]
    return pl.pallas_call(
        functools.partial(_combine_kernel, tm=tm, n_prompt_blocks=npb),
        grid_spec=pltpu.PrefetchScalarGridSpec(
            num_scalar_prefetch=1,
            grid=(n // tm,),
            in_specs=[
                pl.BlockSpec((tm, D_MODEL), row),
                pl.BlockSpec((tm, ROUTE_LANES), row),
                pl.BlockSpec(memory_space=pl.ANY),
                pl.BlockSpec((1, D_MODEL), const),
                pl.BlockSpec((1, D_MODEL), const),
            ],
            out_specs=out_specs,
            scratch_shapes=[pltpu.VMEM((2, 2, tm * TOKEN_ROWS, LANE), jnp.uint32), pltpu.SemaphoreType.DMA((2,))],
        ),
        out_shape=out_shape,
        compiler_params=_params(("arbitrary",)),
        name="moe_combine_ln" if split is None else "moe_combine_ln_final",
    )(pos, x1, rw, ys, g, b)


def kernel(x_prompt, x_sample, state_gla, cache_pool, ln_in_g, ln_in_b, w_in, w_forget_up, b_forget, gla_norm_g, w_pool, pool_scale, w_out, ln1_g, ln1_b, router_group_w, router_group_b, router_expert_w, router_expert_b, w_exp_gate, w_exp_up, w_exp_down, ln2_g, ln2_b):
    n_pb, seq, d = x_prompt.shape
    n_sb, dseq, _ = x_sample.shape
    n_p, n_s = n_pb * seq, n_sb * dseq
    depth = w_in.shape[0]
    row2 = lambda v: v.reshape(1, -1)

    w_out_b = w_out.astype(BF16)
    w_in_t = jnp.swapaxes(w_in, 1, 2)
    wfu = jnp.concatenate([w_forget_up, jnp.zeros((depth, LANE - GATE_RANK, GLA_KW), F32)], axis=1).astype(BF16)
    wpool = w_pool.astype(BF16)
    wr = jnp.concatenate(
        [router_group_w, router_expert_w.transpose(0, 2, 1, 3).reshape(depth, d, N_EXPERTS),
         jnp.zeros((depth, d, ROUTE_LANES - N_GROUPS - N_EXPERTS), F32)], axis=2)
    br = jnp.concatenate(
        [router_group_b, router_expert_b.reshape(depth, N_EXPERTS),
         jnp.zeros((depth, ROUTE_LANES - N_GROUPS - N_EXPERTS), F32)], axis=1)
    w1 = w_exp_gate.reshape(depth * N_EXPERTS, d, EXPERT_HIDDEN)
    w3 = w_exp_up.reshape(depth * N_EXPERTS, d, EXPERT_HIDDEN)
    w2 = w_exp_down.reshape(depth * N_EXPERTS, EXPERT_HIDDEN, d)
    hist0 = jnp.concatenate([jnp.zeros((depth, n_sb, 1, POOL_WIDTH), F32), cache_pool], axis=2)

    x, xb = _ln_in(x_prompt.reshape(n_p, d), x_sample.reshape(n_s, d), row2(ln_in_g), row2(ln_in_b))
    states_p, hists_p, states_s, hists_s = [], [], [], []
    for l in range(depth):
        qkvg = _in_proj_qkvg(xb, w_in_t, l)
        pa = _in_proj_pa(xb, w_in_t, l)
        mix_w = (wfu[l], row2(b_forget[l]), row2(gla_norm_g[l]), wpool[l], row2(pool_scale[l]))
        mixed, sp, hp = _mixer(qkvg, pa, *mix_w, row0=0, n_seq=n_pb, seq_len=seq, n_hist=0)
        mixed, ss, hs = _mixer(qkvg, pa, *mix_w, row0=n_p, n_seq=n_sb, seq_len=dseq, n_hist=POOL_HIST,
                               init=(state_gla[l], hist0[l]), mixed_in=mixed)
        x1, x1p, ri, rw = _out_proj(mixed, x, w_out_b, row2(ln1_g[l]), row2(ln1_b[l]), wr[l], row2(br[l]), l)
        pos, plan = _dispatch_plan(ri)
        xs = _dispatch(pos, x1p)
        ys = _experts(plan, xs, w1, w3, w2, l)
        if l + 1 < depth:
            x, xb = _combine(pos, x1, rw, ys, row2(ln2_g[l]), row2(ln2_b[l]))
        else:
            y_p, y_s = _combine(pos, x1, rw, ys, row2(ln2_g[l]), row2(ln2_b[l]), split=(n_p, n_s))
        states_p.append(sp)
        hists_p.append(hp[:, 1:])
        states_s.append(ss)
        hists_s.append(hs[:, 1:])
    return (y_p.reshape(n_pb, seq, d), y_s.reshape(n_sb, dseq, d),
            jnp.stack(states_p), jnp.stack(hists_p), jnp.stack(states_s), jnp.stack(hists_s))
```

```python
import functools

import jax
import jax.numpy as jnp
from jax import lax
from jax.experimental import pallas as pl
from jax.experimental.pallas import tpu as pltpu

D_MODEL = 2048
DEPTH = 4
CHUNK = 64

GLA_HEADS = 4
GLA_WIDTH = D_MODEL // 2
GLA_DV = GLA_WIDTH // GLA_HEADS
GLA_DK = GLA_DV // 2
GLA_KW = GLA_HEADS * GLA_DK
GATE_RANK = 16
GATE_TAU = 16.0

POOL_WINDOWS = (2, 4, 8, 16)
POOL_GROUPS = len(POOL_WINDOWS)
POOL_WIDTH = D_MODEL - GLA_WIDTH
POOL_GC = POOL_WIDTH // POOL_GROUPS
POOL_HIST = max(POOL_WINDOWS) - 1
HIST_ROWS = POOL_HIST + 1

Q_OFF = 0
K_OFF = Q_OFF + GLA_KW
V_OFF = K_OFF + GLA_KW
G_OFF = V_OFF + GLA_WIDTH
A_OFF = G_OFF + GLA_WIDTH
P_OFF = A_OFF + GATE_RANK
IN_COLS = P_OFF + POOL_WIDTH
PA_COLS = POOL_WIDTH + 128

N_GROUPS = 4
N_EXP = 8
N_EXPERTS = N_GROUPS * N_EXP
EXPERT_HIDDEN = D_MODEL // 4
ROUTE_LANES = 128
ROUTER_CORRECTION = 0.5

DN_ALPHA = (2 * DEPTH) ** 0.25
LN_EPS = 1e-5
RMS_EPS = 1e-6

VMEM_LIMIT_BYTES = 56 * 1024 * 1024
LANE = 128
TOKEN_ROWS = D_MODEL // 2 // LANE
EXPERT_TILE = 256
BF16 = jnp.bfloat16
F32 = jnp.float32


def _params(sem, vmem=VMEM_LIMIT_BYTES):
    return pltpu.CompilerParams(dimension_semantics=sem, vmem_limit_bytes=vmem)


def _row_tile(n, prefs=(640, 512, 320, 256, 128)):
    for t in prefs:
        if n % t == 0:
            return t
    raise ValueError(f"no row tile for {n} rows")


def _layer_norm(x, g, b):
    mu = jnp.mean(x, axis=-1, keepdims=True)
    xc = x - mu
    var = jnp.mean(xc * xc, axis=-1, keepdims=True)
    return xc * lax.rsqrt(var + LN_EPS) * g + b


def _dot(a, b):
    return jnp.dot(a, b, preferred_element_type=F32)


def _dot_nt(a, b):
    return lax.dot_general(a, b, (((1,), (1,)), ((), ())), preferred_element_type=F32)


def _pack_pairs(x):
    k = x.shape[1] // 2
    hi = lax.bitcast_convert_type(x[:, :k].astype(BF16).astype(F32), jnp.uint32)
    lo = lax.bitcast_convert_type(x[:, k:].astype(BF16).astype(F32), jnp.uint32)
    return hi | (lo >> 16)


def _unpack_pairs(w):
    hi = lax.bitcast_convert_type(w & jnp.uint32(0xFFFF0000), F32)
    lo = lax.bitcast_convert_type(w << 16, F32)
    return jnp.concatenate([hi, lo], axis=1)


def _store_token_tiles(ref, words):
    m = words.shape[0]
    for j in range(TOKEN_ROWS):
        ref[pl.ds(j, m, stride=TOKEN_ROWS), :] = words[:, j * LANE:(j + 1) * LANE]


def _load_token_tiles(ref, m):
    return jnp.concatenate([ref[pl.ds(j, m, stride=TOKEN_ROWS), :] for j in range(TOKEN_ROWS)], axis=1)


def _dot_split(m_bf16, x):
    hi = x.astype(BF16)
    lo = (x - hi.astype(F32)).astype(BF16)
    return _dot(m_bf16, hi) + _dot(m_bf16, lo)


def _ln_in_kernel(xp_ref, xs_ref, g_ref, b_ref, x_ref, xb_ref, *, n_prompt_blocks):
    i = pl.program_id(0)
    x = jnp.where(i < n_prompt_blocks, xp_ref[...], xs_ref[...])
    y = _layer_norm(x, g_ref[...], b_ref[...])
    x_ref[...] = y
    xb_ref[...] = y.astype(BF16)


def _ln_in(xp, xs, g, b):
    n_p, n_s = xp.shape[0], xs.shape[0]
    tm = n_s
    assert n_p % tm == 0
    npb = n_p // tm
    n = n_p + n_s
    return pl.pallas_call(
        functools.partial(_ln_in_kernel, n_prompt_blocks=npb),
        grid=(npb + 1,),
        in_specs=[
            pl.BlockSpec((tm, D_MODEL), lambda i: (jnp.minimum(i, npb - 1), 0)),
            pl.BlockSpec((tm, D_MODEL), lambda i: (0, 0)),
            pl.BlockSpec((1, D_MODEL), lambda i: (0, 0)),
            pl.BlockSpec((1, D_MODEL), lambda i: (0, 0)),
        ],
        out_specs=[
            pl.BlockSpec((tm, D_MODEL), lambda i: (i, 0)),
            pl.BlockSpec((tm, D_MODEL), lambda i: (i, 0)),
        ],
        out_shape=[jax.ShapeDtypeStruct((n, D_MODEL), F32), jax.ShapeDtypeStruct((n, D_MODEL), BF16)],
        compiler_params=_params(("arbitrary",)),
        name="ln_in",
    )(xp, xs, g, b)


def _proj_qkvg_kernel(x_ref, wt_ref, o_ref, wb_ref):
    @pl.when(pl.program_id(1) == 0)
    def _():
        wb_ref[...] = wt_ref[...].astype(BF16)

    o_ref[...] = _dot_nt(x_ref[...], wb_ref[...]).astype(o_ref.dtype)


def _in_proj_qkvg(xb, w_in_t, layer):
    n = xb.shape[0]
    tm, tn = _row_tile(n, (1664, 640, 512, 384, 256, 128)), 1024
    return pl.pallas_call(
        _proj_qkvg_kernel,
        grid=(A_OFF // tn, n // tm),
        in_specs=[
            pl.BlockSpec((tm, D_MODEL), lambda j, m: (m, 0)),
            pl.BlockSpec((None, tn, D_MODEL), lambda j, m: (layer, j, 0)),
        ],
        out_specs=pl.BlockSpec((tm, tn), lambda j, m: (m, j)),
        out_shape=jax.ShapeDtypeStruct((n, A_OFF), BF16),
        scratch_shapes=[pltpu.VMEM((tn, D_MODEL), BF16)],
        compiler_params=_params(("arbitrary", "arbitrary")),
        name="in_proj_qkvg",
    )(xb, w_in_t)


def _proj_pa_kernel(x_ref, wt_hbm, o_ref, stage_ref, wb_ref, sem, *, layer):
    n_tail = IN_COLS - A_OFF

    @pl.when(pl.program_id(0) == 0)
    def _():
        copies = (
            pltpu.make_async_copy(wt_hbm.at[layer, pl.ds(P_OFF, POOL_WIDTH)], stage_ref.at[pl.ds(0, POOL_WIDTH)],
                                  sem.at[0]),
            pltpu.make_async_copy(wt_hbm.at[layer, pl.ds(A_OFF, GATE_RANK)],
                                  stage_ref.at[pl.ds(POOL_WIDTH, GATE_RANK)], sem.at[1]),
        )
        for c in copies:
            c.start()
        for c in copies:
            c.wait()
        wb_ref[0:n_tail, :] = stage_ref[...].astype(BF16)
        wb_ref[n_tail:, :] = jnp.zeros((PA_COLS - n_tail, D_MODEL), BF16)

    o_ref[...] = _dot_nt(x_ref[...], wb_ref[...])


def _in_proj_pa(xb, w_in_t, layer):
    n = xb.shape[0]
    tm = _row_tile(n)
    return pl.pallas_call(
        functools.partial(_proj_pa_kernel, layer=layer),
        grid=(n // tm,),
        in_specs=[
            pl.BlockSpec((tm, D_MODEL), lambda m: (m, 0)),
            pl.BlockSpec(memory_space=pl.ANY),
        ],
        out_specs=pl.BlockSpec((tm, PA_COLS), lambda m: (m, 0)),
        out_shape=jax.ShapeDtypeStruct((n, PA_COLS), F32),
        scratch_shapes=[
            pltpu.VMEM((IN_COLS - A_OFF, D_MODEL), F32),
            pltpu.VMEM((PA_COLS, D_MODEL), BF16),
            pltpu.SemaphoreType.DMA((2,)),
        ],
        compiler_params=_params(("arbitrary",)),
        name="in_proj_pa",
    )(xb, w_in_t)


def _mixer_kernel(*refs, n_seq, n_blk, n_chunks, valid, n_hist, has_init, fill_tail):
    if has_init:
        (qkvg_ref, pa_ref, wfu_ref, bf_ref, gain_ref, wpool_ref, pscale_ref, s0_ref, h0_ref, _mixed_in,
         mixed_ref, sout_ref, hout_ref, s_scr, z_scr) = refs
    else:
        (qkvg_ref, pa_ref, wfu_ref, bf_ref, gain_ref, wpool_ref, pscale_ref,
         mixed_ref, sout_ref, hout_ref, s_scr, z_scr) = refs
    C = CHUNK
    step = pl.program_id(0)
    blk = step % n_blk

    def pad_rows(x):
        if valid == C:
            return x
        return jnp.concatenate([x, jnp.zeros((C - valid, x.shape[1]), x.dtype)], axis=0)

    def process():
        @pl.when(blk == 0)
        def _():
            if has_init:
                s_scr[...] = s0_ref[...]
                z_scr[0:C - HIST_ROWS, :] = jnp.zeros((C - HIST_ROWS, POOL_WIDTH), F32)
                z_scr[C - HIST_ROWS:C, :] = h0_ref[...]
            else:
                s_scr[...] = jnp.zeros(s_scr.shape, F32)
                z_scr[...] = jnp.zeros(z_scr.shape, F32)

        row = lax.broadcasted_iota(jnp.int32, (C, C), 0)
        col = lax.broadcasted_iota(jnp.int32, (C, C), 1)
        causal = row >= col
        tri = causal.astype(BF16)
        t_idx = lax.broadcasted_iota(jnp.int32, (C, 2 * C), 0)
        s_idx = lax.broadcasted_iota(jnp.int32, (C, 2 * C), 1) - C
        lag = t_idx - s_idx
        bands = [((lag >= 0) & (lag < w)).astype(BF16) for w in POOL_WINDOWS]

        def chunk(j, carry):
            if n_chunks == 1:
                rows = slice(0, valid)
            else:
                rows = pl.ds(pl.multiple_of(j * C, C), C)
            qkvg = pad_rows(qkvg_ref[rows, :])
            pa = pad_rows(pa_ref[rows, :])
            q = qkvg[:, Q_OFF:K_OFF].astype(F32)
            k = qkvg[:, K_OFF:V_OFF].astype(F32)
            v = qkvg[:, V_OFF:G_OFF]
            g = qkvg[:, G_OFF:A_OFF].astype(F32)
            zc = pa[:, :POOL_WIDTH]
            a = pa[:, POOL_WIDTH:]

            la = jax.nn.log_sigmoid(_dot(a.astype(BF16), wfu_ref[...]) + bf_ref[...]) * (1.0 / GATE_TAU)
            if valid != C:
                la = jnp.where(lax.broadcasted_iota(jnp.int32, la.shape, 0) < valid, la, 0.0)
            bcum = _dot_split(tri, la)
            b_last = bcum[C - 1:C, :]
            q_t = (q * jnp.exp(bcum) * (GLA_DK ** -0.5)).astype(BF16)
            k_t = (k * jnp.exp(-bcum)).astype(BF16)
            k_dec = k * jnp.exp(b_last - bcum)
            e_last = jnp.exp(b_last)
            outs = []
            for h in range(GLA_HEADS):
                ks = slice(h * GLA_DK, (h + 1) * GLA_DK)
                vs = slice(h * GLA_DV, (h + 1) * GLA_DV)
                s_h = s_scr[h]
                att = jnp.where(causal, _dot_nt(q_t[:, ks], k_t[:, ks]), 0.0)
                o = _dot(att.astype(BF16), v[:, vs]) + _dot(q_t[:, ks], s_h.astype(BF16))
                k_aug = jnp.concatenate([k_dec[:, ks], jnp.zeros((GLA_DK - C, GLA_DK), F32)], axis=0)
                v_aug = jnp.concatenate([v[:, vs], jnp.zeros((GLA_DK - C, GLA_DV), BF16)], axis=0)
                decay = jnp.broadcast_to(e_last[:, ks], (GLA_DK, GLA_DK)).T
                decay = jnp.concatenate([decay] * (GLA_DV // GLA_DK), axis=1)
                s_scr[h] = decay * s_h + _dot(k_aug.T.astype(BF16), v_aug)
                outs.append(o * lax.rsqrt(jnp.mean(o * o, axis=-1, keepdims=True) + RMS_EPS))
            gla = jnp.concatenate(outs, axis=1) * gain_ref[...] * (g * jax.nn.sigmoid(g))

            z_ext = jnp.concatenate([z_scr[...], zc], axis=0)
            t0 = (blk * n_chunks + j) * C + n_hist + 1
            avail = lax.broadcasted_iota(jnp.int32, (C, POOL_GC), 0) + t0
            pooled = []
            for gi, w in enumerate(POOL_WINDOWS):
                cs = slice(gi * POOL_GC, (gi + 1) * POOL_GC)
                cnt = jnp.minimum(avail, w).astype(F32)
                m = _dot_split(bands[gi], z_ext[:, cs]) / cnt - zc[:, cs]
                pooled.append(_dot(m.astype(BF16), wpool_ref[gi]))
            pool = jnp.concatenate(pooled, axis=1) * pscale_ref[...]
            z_scr[...] = zc

            res = jnp.concatenate([gla, pool], axis=1).astype(BF16)
            mixed_ref[rows, :] = res[0:valid]
            return carry

        if n_chunks == 1:
            chunk(0, 0)
        else:
            lax.fori_loop(0, n_chunks, chunk, 0, unroll=2)

        @pl.when(blk == n_blk - 1)
        def _():
            sout_ref[...] = s_scr[...]
            hout_ref[...] = z_scr[valid - HIST_ROWS:valid, :]

    if fill_tail:
        pl.when(step < n_seq * n_blk)(process)

        @pl.when(step == n_seq * n_blk)
        def _():
            mixed_ref[...] = jnp.zeros(mixed_ref.shape, BF16)
    else:
        process()


def _mixer_block_kernel(*refs, n_seq, n_blk, rows_valid, n_hist, has_init, fill_tail):
    if has_init:
        (qkvg_ref, pa_ref, wfu_ref, bf_ref, gain_ref, wpool_ref, pscale_ref, s0_ref, h0_ref, _mixed_in,
         mixed_ref, sout_ref, hout_ref, s_scr, z_scr) = refs
    else:
        (qkvg_ref, pa_ref, wfu_ref, bf_ref, gain_ref, wpool_ref, pscale_ref,
         mixed_ref, sout_ref, hout_ref, s_scr, z_scr) = refs
    C = CHUNK
    R = max(rows_valid, 2 * C)
    nc = R // C
    step = pl.program_id(0)
    blk = step % n_blk

    def pad_rows(x):
        if rows_valid == R:
            return x
        return jnp.concatenate([x, jnp.zeros((R - rows_valid, x.shape[1]), x.dtype)], axis=0)

    def process():
        @pl.when(blk == 0)
        def _():
            if has_init:
                s_scr[...] = s0_ref[...]
                z_scr[0:C - HIST_ROWS, :] = jnp.zeros((C - HIST_ROWS, POOL_WIDTH), F32)
                z_scr[C - HIST_ROWS:C, :] = h0_ref[...]
            else:
                s_scr[...] = jnp.zeros(s_scr.shape, F32)
                z_scr[...] = jnp.zeros(z_scr.shape, F32)

        qkvg = pad_rows(qkvg_ref[...])
        pa = pad_rows(pa_ref[...])
        q = qkvg[:, Q_OFF:K_OFF].astype(F32)
        k = qkvg[:, K_OFF:V_OFF].astype(F32)
        v = qkvg[:, V_OFF:G_OFF]
        g = qkvg[:, G_OFF:A_OFF].astype(F32)
        zc = pa[:, :POOL_WIDTH]
        a = pa[:, POOL_WIDTH:]

        row = lax.broadcasted_iota(jnp.int32, (R, R), 0)
        col = lax.broadcasted_iota(jnp.int32, (R, R), 1)
        causal = (row // C == col // C) & (row >= col)
        la = jax.nn.log_sigmoid(_dot(a.astype(BF16), wfu_ref[...]) + bf_ref[...]) * (1.0 / GATE_TAU)
        if rows_valid != R:
            la = jnp.where(lax.broadcasted_iota(jnp.int32, la.shape, 0) < rows_valid, la, 0.0)
        bcum = _dot_split(causal.astype(BF16), la)
        tot = bcum.reshape(nc, C, GLA_KW)[:, C - 1:C, :]
        tot = jnp.broadcast_to(tot, (nc, C, GLA_KW)).reshape(R, GLA_KW)
        q_t = (q * jnp.exp(bcum) * (GLA_DK ** -0.5)).astype(BF16)
        k_t = (k * jnp.exp(-bcum)).astype(BF16)
        k_dec = k * jnp.exp(tot - bcum)
        e_tot = jnp.exp(tot)
        first_half = lax.broadcasted_iota(jnp.int32, (GLA_DK, 2 * C), 1) < C
        outs = []
        for h in range(GLA_HEADS):
            ks = slice(h * GLA_DK, (h + 1) * GLA_DK)
            vs = slice(h * GLA_DV, (h + 1) * GLA_DV)
            att = jnp.where(causal, _dot_nt(q_t[:, ks], k_t[:, ks]), 0.0)
            o_intra = _dot(att.astype(BF16), v[:, vs])
            k_dec_t = k_dec[:, ks].T
            e_tot_t = e_tot[:, ks].T
            updates = []
            for m in range(nc // 2):
                kt_pair = k_dec_t[:, 2 * C * m:2 * C * (m + 1)]
                v_pair = v[2 * C * m:2 * C * (m + 1), vs]
                updates.append(_dot(jnp.where(first_half, kt_pair, 0.0).astype(BF16), v_pair))
                updates.append(_dot(jnp.where(first_half, 0.0, kt_pair).astype(BF16), v_pair))
            s_h = s_scr[h]
            o_inter = []
            for j in range(nc):
                o_inter.append(_dot(q_t[C * j:C * (j + 1), ks], s_h.astype(BF16)))
                decay = jnp.broadcast_to(e_tot_t[:, C * j:C * j + 1], (GLA_DK, GLA_DV))
                s_h = decay * s_h + updates[j]
            s_scr[h] = s_h
            o = o_intra + jnp.concatenate(o_inter, axis=0)
            outs.append(o * lax.rsqrt(jnp.mean(o * o, axis=-1, keepdims=True) + RMS_EPS))
        gla = jnp.concatenate(outs, axis=1) * gain_ref[...] * (g * jax.nn.sigmoid(g))

        z_all = jnp.concatenate([z_scr[...], zc], axis=0)
        z_hi = z_all.astype(BF16)
        z_lo = (z_all - z_hi.astype(F32)).astype(BF16)
        t_idx = lax.broadcasted_iota(jnp.int32, (C, 2 * C), 0)
        s_idx = lax.broadcasted_iota(jnp.int32, (C, 2 * C), 1) - C
        lag = t_idx - s_idx
        avail0 = lax.broadcasted_iota(jnp.int32, (C, POOL_GC), 0) + (blk * rows_valid + n_hist + 1)
        pooled = []
        for gi, w in enumerate(POOL_WINDOWS):
            cs = slice(gi * POOL_GC, (gi + 1) * POOL_GC)
            band = ((lag >= 0) & (lag < w)).astype(BF16)
            means = []
            for j in range(nc):
                win = slice(C * j, C * (j + 2))
                sums = _dot(band, z_hi[win, cs]) + _dot(band, z_lo[win, cs])
                cnt = jnp.minimum(avail0 + C * j, w).astype(F32)
                means.append(sums / cnt - zc[C * j:C * (j + 1), cs])
            pooled.append(_dot(jnp.concatenate(means, axis=0).astype(BF16), wpool_ref[gi]))
        pool = jnp.concatenate(pooled, axis=1) * pscale_ref[...]
        z_scr[...] = zc[R - C:R, :]

        res = jnp.concatenate([gla, pool], axis=1).astype(BF16)
        mixed_ref[...] = res[0:rows_valid]

        @pl.when(blk == n_blk - 1)
        def _():
            sout_ref[...] = s_scr[...]
            hout_ref[...] = zc[rows_valid - HIST_ROWS:rows_valid, :]

    if fill_tail:
        pl.when(step < n_seq * n_blk)(process)

        @pl.when(step == n_seq * n_blk)
        def _():
            mixed_ref[...] = jnp.zeros(mixed_ref.shape, BF16)
    else:
        process()


def _mixer(qkvg, pa, wfu, bfg, gain, wpool, pscale, *, row0, n_seq, seq_len, n_hist, init=None, mixed_in=None):
    n = qkvg.shape[0]
    if seq_len % (2 * CHUNK) == 0:
        rb = _row_tile(seq_len, (512, 256, 128))
    else:
        rb = seq_len
        assert HIST_ROWS <= seq_len < CHUNK and init is not None
    assert seq_len % rb == 0 and row0 % rb == 0
    n_blk = seq_len // rb
    base = row0 // rb
    n_steps = n_seq * n_blk
    has_init = init is not None
    assert has_init == (mixed_in is not None)
    tail_rows = n - (row0 + n_seq * seq_len)
    fill_tail = (not has_init) and tail_rows > 0
    assert tail_rows <= rb

    rows = lambda t: (base + t, 0)
    const2 = lambda t: (0, 0)
    seq3 = lambda t: (jnp.minimum(t // n_blk, n_seq - 1), 0, 0)
    seq4 = lambda t: (jnp.minimum(t // n_blk, n_seq - 1), 0, 0, 0)

    in_specs = [
        pl.BlockSpec((rb, A_OFF), rows),
        pl.BlockSpec((rb, PA_COLS), rows),
        pl.BlockSpec((LANE, GLA_KW), const2),
        pl.BlockSpec((1, GLA_KW), const2),
        pl.BlockSpec((1, GLA_WIDTH), const2),
        pl.BlockSpec((POOL_GROUPS, POOL_GC, POOL_GC), lambda t: (0, 0, 0)),
        pl.BlockSpec((1, POOL_WIDTH), const2),
    ]
    args = [qkvg, pa, wfu, bfg, gain, wpool, pscale]
    aliases = {}
    if has_init:
        in_specs += [
            pl.BlockSpec((None, GLA_HEADS, GLA_DK, GLA_DV), seq4),
            pl.BlockSpec((None, HIST_ROWS, POOL_WIDTH), seq3),
            pl.BlockSpec(memory_space=pl.ANY),
        ]
        args += [init[0], init[1], mixed_in]
        aliases = {len(args) - 1: 0}
    return pl.pallas_call(
        functools.partial(_mixer_block_kernel, n_seq=n_seq, n_blk=n_blk, rows_valid=rb, n_hist=n_hist,
                          has_init=has_init, fill_tail=fill_tail),
        grid=(n_steps + int(fill_tail),),
        in_specs=in_specs,
        out_specs=[
            pl.BlockSpec((rb, D_MODEL), rows),
            pl.BlockSpec((None, GLA_HEADS, GLA_DK, GLA_DV), seq4),
            pl.BlockSpec((None, HIST_ROWS, POOL_WIDTH), seq3),
        ],
        out_shape=[
            jax.ShapeDtypeStruct((n, D_MODEL), BF16),
            jax.ShapeDtypeStruct((n_seq, GLA_HEADS, GLA_DK, GLA_DV), F32),
            jax.ShapeDtypeStruct((n_seq, HIST_ROWS, POOL_WIDTH), F32),
        ],
        scratch_shapes=[
            pltpu.VMEM((GLA_HEADS, GLA_DK, GLA_DV), F32),
            pltpu.VMEM((CHUNK, POOL_WIDTH), F32),
        ],
        input_output_aliases=aliases,
        compiler_params=_params(("arbitrary",)),
        name="mixer_init" if has_init else "mixer",
    )(*args)


def _route(logits):
    lane = lax.broadcasted_iota(jnp.int32, logits.shape, 1).astype(F32)
    neg = -jnp.inf
    first_lane = lambda hit: jnp.min(jnp.where(hit, lane, float(ROUTE_LANES)), axis=-1, keepdims=True)
    lg = jnp.where(lane < N_GROUPS, logits, neg)
    un = jnp.exp(lg - jnp.max(lg, axis=-1, keepdims=True))
    pg = un / jnp.sum(un, axis=-1, keepdims=True)
    top_pg = jnp.max(pg, axis=-1, keepdims=True)
    gsel = first_lane((pg == top_pg) & (lane < N_GROUPS))
    first = N_GROUPS + gsel * N_EXP
    le = jnp.where((lane >= first) & (lane < first + N_EXP), logits, neg)
    v0 = jnp.max(le, axis=-1, keepdims=True)
    i0 = first_lane(le == v0)
    le = jnp.where(lane == i0, neg, le)
    v1 = jnp.max(le, axis=-1, keepdims=True)
    i1 = first_lane(le == v1)
    u1 = jnp.exp(v1 - v0)
    den = 1.0 + u1
    e0 = (i0 - N_GROUPS).astype(jnp.int32)
    e1 = (i1 - N_GROUPS).astype(jnp.int32)
    return e0, e1, (1.0 / den) * top_pg, (u1 / den) * top_pg


def _out_proj_kernel(mixed_ref, x_ref, w_ref, g_ref, b_ref, wr_ref, br_ref, x1_ref, x1p_ref, ri_ref, rw_ref,
                     wrs_ref):
    tm = x_ref.shape[0]

    @pl.when(pl.program_id(0) == 0)
    def _():
        wr = wr_ref[...]
        wr_hi = wr.astype(BF16)
        wrs_ref[:, :ROUTE_LANES] = wr_hi
        wrs_ref[:, ROUTE_LANES:] = (wr - wr_hi.astype(F32)).astype(BF16)

    y = _dot(mixed_ref[...], w_ref[...])
    x1 = _layer_norm(DN_ALPHA * x_ref[...] + y, g_ref[...], b_ref[...])
    x1_ref[...] = x1
    _store_token_tiles(x1p_ref, _pack_pairs(x1))
    x_hi = x1.astype(BF16)
    x_lo = (x1 - x_hi.astype(F32)).astype(BF16)
    prod = _dot(jnp.concatenate([x_hi, x_lo], axis=0), wrs_ref[...])
    corr = prod[tm:, :ROUTE_LANES] + prod[:tm, ROUTE_LANES:]
    logits = prod[:tm, :ROUTE_LANES] + ROUTER_CORRECTION * corr + br_ref[...]
    e0, e1, w0, w1 = _route(logits)
    lane = lax.broadcasted_iota(jnp.int32, logits.shape, 1)
    ri_ref[...] = jnp.where(lane == 0, e0, jnp.where(lane == 1, e1, 0))
    rw_ref[...] = jnp.where(lane == 0, w0, jnp.where(lane == 1, w1, 0.0))


def _out_proj(mixed, x, w_out, g, b, wr, br, layer):
    n = x.shape[0]
    tm = _row_tile(n, (640, 416, 320, 256, 128))
    row = lambda i: (i, 0)
    const = lambda i: (0, 0)
    return pl.pallas_call(
        _out_proj_kernel,
        grid=(n // tm,),
        in_specs=[
            pl.BlockSpec((tm, D_MODEL), row),
            pl.BlockSpec((tm, D_MODEL), row),
            pl.BlockSpec((None, D_MODEL, D_MODEL), lambda i: (layer, 0, 0), pipeline_mode=pl.Buffered(1)),
            pl.BlockSpec((1, D_MODEL), const),
            pl.BlockSpec((1, D_MODEL), const),
            pl.BlockSpec((D_MODEL, ROUTE_LANES), const),
            pl.BlockSpec((1, ROUTE_LANES), const),
        ],
        out_specs=[
            pl.BlockSpec((tm, D_MODEL), row),
            pl.BlockSpec((tm * TOKEN_ROWS, LANE), row),
            pl.BlockSpec((tm, ROUTE_LANES), row),
            pl.BlockSpec((tm, ROUTE_LANES), row),
        ],
        out_shape=[
            jax.ShapeDtypeStruct((n, D_MODEL), F32),
            jax.ShapeDtypeStruct((n * TOKEN_ROWS, LANE), jnp.uint32),
            jax.ShapeDtypeStruct((n, ROUTE_LANES), jnp.int32),
            jax.ShapeDtypeStruct((n, ROUTE_LANES), F32),
        ],
        scratch_shapes=[pltpu.VMEM((D_MODEL, 2 * ROUTE_LANES), BF16)],
        compiler_params=_params(("arbitrary",)),
        name="out_proj_ln_route",
    )(mixed, x, w_out, g, b, wr, br)


def _sort_kernel(ri_ref, pos_ref, cnt_ref, *, tb):
    n = ri_ref.shape[0]
    n_blocks = n // tb
    lane = lax.broadcasted_iota(jnp.int32, (tb, ROUTE_LANES), 1)
    ones = jnp.ones((8, tb), BF16)

    def onehots(b):
        ri = ri_ref[pl.ds(pl.multiple_of(b * tb, tb), tb), :]
        return lane == ri[:, 0:1], lane == ri[:, 1:2]

    def count(b, acc):
        h0, h1 = onehots(b)
        return acc + _dot(ones, jnp.where(h0 | h1, 1.0, 0.0).astype(BF16))

    counts = lax.fori_loop(0, n_blocks, count, jnp.zeros((8, ROUTE_LANES), F32))
    cnt_ref[...] = counts.astype(jnp.int32)
    before = (lax.broadcasted_iota(jnp.int32, (ROUTE_LANES, ROUTE_LANES), 0)
              < lax.broadcasted_iota(jnp.int32, (ROUTE_LANES, ROUTE_LANES), 1)).astype(BF16)
    c_hi = jnp.floor(counts * (1.0 / ROUTE_LANES))
    c_lo = counts - c_hi * ROUTE_LANES
    offs = (_dot(c_hi.astype(BF16), before) * ROUTE_LANES + _dot(c_lo.astype(BF16), before))[0:1, :]
    earlier = (lax.broadcasted_iota(jnp.int32, (tb, tb), 0) > lax.broadcasted_iota(jnp.int32, (tb, tb), 1)).astype(BF16)

    def place(b, carry):
        h0, h1 = onehots(b)
        both = jnp.where(h0 | h1, 1.0, 0.0).astype(BF16)
        start = _dot(earlier, both) + carry + offs
        p0 = jnp.sum(jnp.where(h0, start, 0.0), axis=-1, keepdims=True)
        p1 = jnp.sum(jnp.where(h1, start, 0.0), axis=-1, keepdims=True)
        out = jnp.where(lane == 0, p0, jnp.where(lane == 1, p1, 0.0))
        pos_ref[pl.ds(pl.multiple_of(b * tb, tb), tb), :] = out.astype(jnp.int32)
        return carry + _dot(ones, both)[0:1, :]

    lax.fori_loop(0, n_blocks, place, jnp.zeros((1, ROUTE_LANES), F32))


def _sort_pairs(ri):
    n = ri.shape[0]
    tb = _row_tile(n, (640, 512, 384, 256, 128))
    pos, cnt = pl.pallas_call(
        functools.partial(_sort_kernel, tb=tb),
        out_shape=[jax.ShapeDtypeStruct((n, ROUTE_LANES), jnp.int32), jax.ShapeDtypeStruct((8, ROUTE_LANES), jnp.int32)],
        compiler_params=pltpu.CompilerParams(vmem_limit_bytes=VMEM_LIMIT_BYTES),
        name="moe_sort",
    )(ri)
    return pos, cnt[0, :N_EXPERTS]


def _dispatch_plan(ri):
    n_pairs = ri.shape[0] * 2
    n_tiles = n_pairs // EXPERT_TILE
    pos, counts = _sort_pairs(ri)
    pos = pos[:, :2].reshape(-1)
    offs = jnp.concatenate([jnp.zeros((1,), jnp.int32), jnp.cumsum(counts)])
    bounds = jnp.sort(jnp.concatenate([jnp.arange(n_tiles, dtype=jnp.int32) * EXPERT_TILE, offs[:N_EXPERTS]]))
    seg_lo = bounds
    seg_hi = jnp.concatenate([bounds[1:], jnp.full((1,), n_pairs, jnp.int32)])
    tile = jnp.minimum(seg_lo // EXPERT_TILE, n_tiles - 1)
    expert = jnp.minimum(jnp.searchsorted(offs[1:], seg_lo, side="right"), N_EXPERTS - 1).astype(jnp.int32)
    lo = seg_lo - tile * EXPERT_TILE
    hi = jnp.where(seg_hi > seg_lo, seg_hi - tile * EXPERT_TILE, lo)
    valid = seg_hi > seg_lo
    e_seen = lax.cummax(jnp.where(valid, expert, -1))
    prev = jnp.concatenate([jnp.full((1,), -1, jnp.int32), e_seen[:-1]])
    fetch = valid & (expert > prev)
    run = jnp.maximum(jnp.cumsum(fetch.astype(jnp.int32)) - 1, 0)
    slot = run % 2
    fetch = fetch.astype(jnp.int32) * jnp.where(run == 0, 2, 1)
    ids = jnp.arange(N_EXPERTS, dtype=jnp.int32)
    later = lax.cummin(jnp.where(counts > 0, ids, N_EXPERTS), reverse=True)
    nxt_of = jnp.concatenate([later[1:], jnp.full((1,), N_EXPERTS, jnp.int32)])
    nxt = jnp.where(nxt_of[expert] < N_EXPERTS, nxt_of[expert], -1)
    sched = (fetch, slot.astype(jnp.int32), nxt.astype(jnp.int32))
    return pos, (tile.astype(jnp.int32), expert, lo.astype(jnp.int32), hi.astype(jnp.int32)) + sched


def _dispatch_kernel(pos_ref, x_ref, xs_hbm, sem, *, tb):
    base = pl.program_id(0) * tb

    def scatter(t, k):
        dst = pl.multiple_of(pos_ref[2 * (base + t) + k] * TOKEN_ROWS, TOKEN_ROWS)
        src = pl.multiple_of(t * TOKEN_ROWS, TOKEN_ROWS)
        return pltpu.make_async_copy(x_ref.at[pl.ds(src, TOKEN_ROWS)], xs_hbm.at[pl.ds(dst, TOKEN_ROWS)], sem)

    def issue(t, c):
        scatter(t, 0).start(priority=0)
        scatter(t, 1).start(priority=1)
        return c

    lax.fori_loop(0, tb, issue, 0, unroll=8)

    def drain(t, c):
        scatter(t, 0).wait()
        scatter(t, 1).wait()
        return c

    lax.fori_loop(0, tb, drain, 0, unroll=8)


def _dispatch(pos, x1):
    n = x1.shape[0] // TOKEN_ROWS
    tb = _row_tile(n)
    return pl.pallas_call(
        functools.partial(_dispatch_kernel, tb=tb),
        grid_spec=pltpu.PrefetchScalarGridSpec(
            num_scalar_prefetch=1,
            grid=(n // tb,),
            in_specs=[pl.BlockSpec((tb * TOKEN_ROWS, LANE), lambda i, p: (i, 0))],
            out_specs=pl.BlockSpec(memory_space=pl.ANY),
            scratch_shapes=[pltpu.SemaphoreType.DMA(())],
        ),
        out_shape=jax.ShapeDtypeStruct((2 * n * TOKEN_ROWS, LANE), x1.dtype),
        compiler_params=_params(("arbitrary",)),
        name="moe_dispatch",
    )(pos, x1)


def _experts_kernel(tile_ref, exp_ref, lo_ref, hi_ref, fetch_ref, slot_ref, nxt_ref,
                    xs_ref, w1_hbm, w3_hbm, w2_hbm, o_ref, w1_buf, w3_buf, w2_buf, sem, wb1, wb3, wb2, *, layer):
    i = pl.program_id(0)
    lo, hi = lo_ref[i], hi_ref[i]
    slot = slot_ref[i]

    def weight_copies(expert, s):
        e = layer * N_EXPERTS + expert
        return (pltpu.make_async_copy(w1_hbm.at[e], w1_buf.at[s], sem.at[s, 0]),
                pltpu.make_async_copy(w3_hbm.at[e], w3_buf.at[s], sem.at[s, 1]),
                pltpu.make_async_copy(w2_hbm.at[e], w2_buf.at[s], sem.at[s, 2]))

    @pl.when(fetch_ref[i] == 2)
    def _():
        for c in weight_copies(exp_ref[i], slot):
            c.start()

    @pl.when(fetch_ref[i] > 0)
    def _():
        for c in weight_copies(exp_ref[i], slot):
            c.wait()
        wb1[...] = w1_buf[slot].astype(BF16)
        wb3[...] = w3_buf[slot].astype(BF16)
        wb2[...] = w2_buf[slot].astype(BF16)

        @pl.when(nxt_ref[i] >= 0)
        def _():
            for c in weight_copies(nxt_ref[i], 1 - slot):
                c.start()

    def compute():
        x = _unpack_pairs(_load_token_tiles(xs_ref, EXPERT_TILE)).astype(BF16)
        h1 = _dot(x, wb1[...])
        h3 = _dot(x, wb3[...])
        a = (h1 * jax.nn.sigmoid(h1) * h3).astype(BF16)
        return _pack_pairs(_dot(a, wb2[...]))

    @pl.when((hi > lo) & (lo == 0))
    def _():
        _store_token_tiles(o_ref, compute())

    @pl.when((hi > lo) & (lo > 0))
    def _():
        r = lax.broadcasted_iota(jnp.int32, (EXPERT_TILE, D_MODEL // 2), 0)
        old = _load_token_tiles(o_ref, EXPERT_TILE)
        _store_token_tiles(o_ref, jnp.where((r >= lo) & (r < hi), compute(), old))


def _experts(plan, xs, w1, w3, w2, layer):
    n_items = plan[0].shape[0]
    xrow = lambda i, t, *_: (t[i], 0)
    any_spec = pl.BlockSpec(memory_space=pl.ANY)
    return pl.pallas_call(
        functools.partial(_experts_kernel, layer=layer),
        grid_spec=pltpu.PrefetchScalarGridSpec(
            num_scalar_prefetch=len(plan),
            grid=(n_items,),
            in_specs=[pl.BlockSpec((EXPERT_TILE * TOKEN_ROWS, LANE), xrow), any_spec, any_spec, any_spec],
            out_specs=pl.BlockSpec((EXPERT_TILE * TOKEN_ROWS, LANE), xrow),
            scratch_shapes=[
                pltpu.VMEM((2, D_MODEL, EXPERT_HIDDEN), F32),
                pltpu.VMEM((2, D_MODEL, EXPERT_HIDDEN), F32),
                pltpu.VMEM((2, EXPERT_HIDDEN, D_MODEL), F32),
                pltpu.SemaphoreType.DMA((2, 3)),
                pltpu.VMEM((D_MODEL, EXPERT_HIDDEN), BF16),
                pltpu.VMEM((D_MODEL, EXPERT_HIDDEN), BF16),
                pltpu.VMEM((EXPERT_HIDDEN, D_MODEL), BF16),
            ],
        ),
        out_shape=jax.ShapeDtypeStruct(xs.shape, xs.dtype),
        compiler_params=_params(("arbitrary",)),
        name="moe_experts",
    )(*plan, xs, w1, w3, w2)


def _combine_kernel(pos_ref, x1_ref, rw_ref, ys_hbm, g_ref, b_ref, *rest, tm, n_prompt_blocks):
    if n_prompt_blocks is None:
        x2_ref, x2b_ref, buf, sem = rest
    else:
        yp_ref, ysm_ref, buf, sem = rest
    i = pl.program_id(0)
    slot = i % 2

    def gather(step, s, t, k):
        src = pl.multiple_of(pos_ref[2 * (step * tm + t) + k] * TOKEN_ROWS, TOKEN_ROWS)
        dst = pl.multiple_of(t * TOKEN_ROWS, TOKEN_ROWS)
        return pltpu.make_async_copy(ys_hbm.at[pl.ds(src, TOKEN_ROWS)], buf.at[s, k, pl.ds(dst, TOKEN_ROWS)],
                                     sem.at[s])

    def issue(step, s):
        def body(t, c):
            gather(step, s, t, 0).start(priority=0)
            gather(step, s, t, 1).start(priority=1)
            return c

        lax.fori_loop(0, tm, body, 0, unroll=8)

    @pl.when(i == 0)
    def _():
        issue(0, 0)

    @pl.when(i + 1 < pl.num_programs(0))
    def _():
        issue(i + 1, 1 - slot)

    def drain(t, c):
        gather(i, slot, t, 0).wait()
        gather(i, slot, t, 1).wait()
        return c

    lax.fori_loop(0, tm, drain, 0, unroll=8)

    rw = rw_ref[...]
    f = (rw[:, 0:1] * _unpack_pairs(_load_token_tiles(buf.at[slot, 0], tm))
         + rw[:, 1:2] * _unpack_pairs(_load_token_tiles(buf.at[slot, 1], tm)))
    x2 = _layer_norm(DN_ALPHA * x1_ref[...] + f, g_ref[...], b_ref[...])
    if n_prompt_blocks is None:
        x2_ref[...] = x2
        x2b_ref[...] = x2.astype(BF16)
    else:
        @pl.when(i < n_prompt_blocks)
        def _():
            yp_ref[...] = x2

        @pl.when(i >= n_prompt_blocks)
        def _():
            ysm_ref[...] = x2


def _combine(pos, x1, rw, ys, g, b, *, split=None):
    n = x1.shape[0]
    row = lambda i, p: (i, 0)
    const = lambda i, p: (0, 0)
    if split is None:
        tm, npb = _row_tile(n, (320, 256, 128)), None
        out_specs = [pl.BlockSpec((tm, D_MODEL), row), pl.BlockSpec((tm, D_MODEL), row)]
        out_shape = [jax.ShapeDtypeStruct((n, D_MODEL), F32), jax.ShapeDtypeStruct((n, D_MODEL), BF16)]
    else:
        n_p, n_s = split
        tm = n_s
        assert n_p % tm == 0 and n_p + n_s == n
        npb = n_p // tm
        out_specs = [
            pl.BlockSpec((tm, D_MODEL), lambda i, p: (jnp.minimum(i, npb - 1), 0)),
            pl.BlockSpec((tm, D_MODEL), lambda i, p: (jnp.maximum(i - npb, 0), 0)),
        ]
        out_shape = [jax.ShapeDtypeStruct((n_p, D_MODEL), F32), jax.ShapeDtypeStruct((n_s, D_MODEL), F32)]
    return pl.pallas_call(
        functools.partial(_combine_kernel, tm=tm, n_prompt_blocks=npb),
        grid_spec=pltpu.PrefetchScalarGridSpec(
            num_scalar_prefetch=1,
            grid=(n // tm,),
            in_specs=[
                pl.BlockSpec((tm, D_MODEL), row),
                pl.BlockSpec((tm, ROUTE_LANES), row),
                pl.BlockSpec(memory_space=pl.ANY),
                pl.BlockSpec((1, D_MODEL), const),
                pl.BlockSpec((1, D_MODEL), const),
            ],
            out_specs=out_specs,
            scratch_shapes=[pltpu.VMEM((2, 2, tm * TOKEN_ROWS, LANE), jnp.uint32), pltpu.SemaphoreType.DMA((2,))],
        ),
        out_shape=out_shape,
        compiler_params=_params(("arbitrary",)),
        name="moe_combine_ln" if split is None else "moe_combine_ln_final",
    )(pos, x1, rw, ys, g, b)


def kernel(x_prompt, x_sample, state_gla, cache_pool, ln_in_g, ln_in_b, w_in, w_forget_up, b_forget, gla_norm_g, w_pool, pool_scale, w_out, ln1_g, ln1_b, router_group_w, router_group_b, router_expert_w, router_expert_b, w_exp_gate, w_exp_up, w_exp_down, ln2_g, ln2_b):
    n_pb, seq, d = x_prompt.shape
    n_sb, dseq, _ = x_sample.shape
    n_p, n_s = n_pb * seq, n_sb * dseq
    depth = w_in.shape[0]
    row2 = lambda v: v.reshape(1, -1)

    w_out_b = w_out.astype(BF16)
    w_in_t = jnp.swapaxes(w_in, 1, 2)
    wfu = jnp.concatenate([w_forget_up, jnp.zeros((depth, LANE - GATE_RANK, GLA_KW), F32)], axis=1).astype(BF16)
    wpool = w_pool.astype(BF16)
    wr = jnp.concatenate(
        [router_group_w, router_expert_w.transpose(0, 2, 1, 3).reshape(depth, d, N_EXPERTS),
         jnp.zeros((depth, d, ROUTE_LANES - N_GROUPS - N_EXPERTS), F32)], axis=2)
    br = jnp.concatenate(
        [router_group_b, router_expert_b.reshape(depth, N_EXPERTS),
         jnp.zeros((depth, ROUTE_LANES - N_GROUPS - N_EXPERTS), F32)], axis=1)
    w1 = w_exp_gate.reshape(depth * N_EXPERTS, d, EXPERT_HIDDEN)
    w3 = w_exp_up.reshape(depth * N_EXPERTS, d, EXPERT_HIDDEN)
    w2 = w_exp_down.reshape(depth * N_EXPERTS, EXPERT_HIDDEN, d)
    hist0 = jnp.concatenate([jnp.zeros((depth, n_sb, 1, POOL_WIDTH), F32), cache_pool], axis=2)

    x, xb = _ln_in(x_prompt.reshape(n_p, d), x_sample.reshape(n_s, d), row2(ln_in_g), row2(ln_in_b))
    states_p, hists_p, states_s, hists_s = [], [], [], []
    for l in range(depth):
        qkvg = _in_proj_qkvg(xb, w_in_t, l)
        pa = _in_proj_pa(xb, w_in_t, l)
        mix_w = (wfu[l], row2(b_forget[l]), row2(gla_norm_g[l]), wpool[l], row2(pool_scale[l]))
        mixed, sp, hp = _mixer(qkvg, pa, *mix_w, row0=0, n_seq=n_pb, seq_len=seq, n_hist=0)
        mixed, ss, hs = _mixer(qkvg, pa, *mix_w, row0=n_p, n_seq=n_sb, seq_len=dseq, n_hist=POOL_HIST,
                               init=(state_gla[l], hist0[l]), mixed_in=mixed)
        x1, x1p, ri, rw = _out_proj(mixed, x, w_out_b, row2(ln1_g[l]), row2(ln1_b[l]), wr[l], row2(br[l]), l)
        pos, plan = _dispatch_plan(ri)
        xs = _dispatch(pos, x1p)
        ys = _experts(plan, xs, w1, w3, w2, l)
        if l + 1 < depth:
            x, xb = _combine(pos, x1, rw, ys, row2(ln2_g[l]), row2(ln2_b[l]))
        else:
            y_p, y_s = _combine(pos, x1, rw, ys, row2(ln2_g[l]), row2(ln2_b[l]), split=(n_p, n_s))
        states_p.append(sp)
        hists_p.append(hp[:, 1:])
        states_s.append(ss)
        hists_s.append(hs[:, 1:])
    return (y_p.reshape(n_pb, seq, d), y_s.reshape(n_sb, dseq, d),
            jnp.stack(states_p), jnp.stack(hists_p), jnp.stack(states_s), jnp.stack(hists_s))
```
